```python
import math
import jax, jax.numpy as jnp
from jax import lax
import numpy as np

D_MODEL = 1024
BATCH = 8
SEQ = 2048
DEPTH = 4
DEC_BATCH = 128
DEC_SEQ = 4
PAST_LEN = 2048
PAGE_SIZE = 128

SSD_EXPAND = 2
D_INNER = SSD_EXPAND * D_MODEL
SSD_HEAD_DIM = 64
N_SSD_HEADS = D_INNER // SSD_HEAD_DIM
D_STATE = 128
N_GROUPS = 8
SSD_CONV_W = 4
CONV_DIM = D_INNER + 2 * N_GROUPS * D_STATE
SSD_CHUNK = 128
ATT_HEAD_DIM = 64
N_ATT_HEADS = D_MODEL // ATT_HEAD_DIM
D_ATT = N_ATT_HEADS * ATT_HEAD_DIM
Q_BLOCK = 128
SB_BIAS_INIT = -6.0
D_FF = ((8 * D_MODEL // 3 + 127) // 128) * 128
FFN_CONV_W = 3
N_MOD = 6
EPS = 1e-6
IN_SPLITS = (D_INNER, CONV_DIM, N_SSD_HEADS, D_ATT, D_ATT, D_ATT, D_MODEL, D_MODEL)
IN_COLS = sum(IN_SPLITS)

kernel_name = 'hybrid_ssd_stickbreaking_convffn_adaln_step'


def rmsnorm(x, g):
    xf = x.astype(jnp.float32)
    y = xf * lax.rsqrt(jnp.mean(xf * xf, axis=-1, keepdims=True) + EPS)
    return y.astype(x.dtype) * g


def modulate(x, g, shift, scale):
    return rmsnorm(x, g) * (1 + scale[:, None]) + shift[:, None]


def causal_dwconv(x, w, b, prev):
    width, t = w.shape[0], x.shape[1]
    xp = jnp.concatenate([prev.astype(x.dtype), x], axis=1)
    out = b + sum(xp[:, i:i + t] * w[i] for i in range(width))
    return out, xp[:, t:]


def ssd_scan(x, dt, a, bm, cm, h0, chunk):
    b, l, h, p = x.shape
    g, n = bm.shape[2], bm.shape[3]
    r = h // g
    c = l // chunk
    f32 = jnp.float32
    xdt = (x.astype(f32) * dt[..., None]).reshape(b, c, chunk, g, r, p)
    acs = jnp.cumsum((dt * a).reshape(b, c, chunk, g, r), axis=2)
    B = bm.astype(f32).reshape(b, c, chunk, g, n)
    C = cm.astype(f32).reshape(b, c, chunk, g, n)
    causal = jnp.tril(jnp.ones((chunk, chunk), dtype=bool))[:, :, None, None]
    diff = acs[:, :, :, None] - acs[:, :, None, :]
    decay = jnp.exp(jnp.where(causal, diff, -jnp.inf))
    cb = jnp.einsum('bctgn,bcsgn->bctsg', C, B)
    y_diag = jnp.einsum('bctsg,bctsgr,bcsgrp->bctgrp', cb, decay, xdt)
    decay_end = jnp.exp(acs[:, :, -1:] - acs)
    states = jnp.einsum('bcsgn,bcsgr,bcsgrp->bcgrpn', B, decay_end, xdt)
    chunk_decay = jnp.exp(acs[:, :, -1])

    def step(hc, inp):
        st, dec = inp
        return hc * dec[..., None, None] + st, hc

    h_final, h_prev = lax.scan(step, h0.astype(f32).reshape(b, g, r, p, n),
                               (jnp.moveaxis(states, 1, 0), jnp.moveaxis(chunk_decay, 1, 0)))
    h_prev = jnp.moveaxis(h_prev, 0, 1)
    y_off = jnp.einsum('bctgn,bcgrpn,bctgr->bctgrp', C, h_prev, jnp.exp(acs))
    y = (y_diag + y_off).reshape(b, l, h, p)
    return y, h_final.reshape(b, h, p, n)


def sb_block(q, qpos, k, v, kpos, bias):
    z = jnp.einsum('bqhd,bkhd->bhqk', q, k, preferred_element_type=jnp.float32) * (q.shape[-1] ** -0.5)
    z = z + bias.astype(jnp.float32)[None, :, None, None]
    mask = kpos[None, :] < qpos[:, None]
    log_keep = jnp.where(mask, jax.nn.log_sigmoid(-z), 0.0)
    suffix = lax.cumsum(log_keep, axis=3, reverse=True) - log_keep
    w = jnp.where(mask, jnp.exp(jax.nn.log_sigmoid(z) + suffix), 0.0)
    return jnp.einsum('bhqk,bkhd->bqhd', w.astype(v.dtype), v)


def sb_attention_prompt(q, k, v, bias):
    b, s, h, d = q.shape
    nb = s // Q_BLOCK
    qb = q.reshape(b, nb, Q_BLOCK, h, d).transpose(1, 0, 2, 3, 4)
    kpos = jnp.arange(s)

    def one(args):
        qi, i = args
        return sb_block(qi, i * Q_BLOCK + jnp.arange(Q_BLOCK), k, v, kpos, bias)

    out = lax.map(one, (qb, jnp.arange(nb)))
    return out.transpose(1, 0, 2, 3, 4).reshape(b, s, h, d)


def token_mixer(h, lp, conv_prev, ssm_prev, past_k, past_v):
    b, t, _ = h.shape
    proj = h @ lp['w_in']
    cuts = [int(i) for i in np.cumsum(IN_SPLITS)[:-1]]
    z, xbc, dt_raw, q, k, v, g_ssd, g_att = jnp.split(proj, cuts, axis=-1)
    xbc, conv_new = causal_dwconv(xbc, lp['conv_w'], lp['conv_b'], conv_prev)
    xbc = jax.nn.silu(xbc)
    xs, bm, cm = jnp.split(xbc, [D_INNER, D_INNER + N_GROUPS * D_STATE], axis=-1)
    xs = xs.reshape(b, t, N_SSD_HEADS, SSD_HEAD_DIM)
    bm = bm.reshape(b, t, N_GROUPS, D_STATE)
    cm = cm.reshape(b, t, N_GROUPS, D_STATE)
    dt = jax.nn.softplus(dt_raw.astype(jnp.float32) + lp['dt_bias'].astype(jnp.float32))
    a = -jnp.exp(lp['a_log'].astype(jnp.float32))
    y, ssm_new = ssd_scan(xs, dt, a, bm, cm, ssm_prev, math.gcd(SSD_CHUNK, t))
    y = y.astype(h.dtype) + lp['d_skip'][:, None] * xs
    y = rmsnorm(y.reshape(b, t, D_INNER) * jax.nn.silu(z), lp['ssd_norm'])
    q = q.reshape(b, t, N_ATT_HEADS, ATT_HEAD_DIM)
    k = k.reshape(b, t, N_ATT_HEADS, ATT_HEAD_DIM)
    v = v.reshape(b, t, N_ATT_HEADS, ATT_HEAD_DIM)
    if past_k is None:
        o = sb_attention_prompt(q, k, v, lp['sb_bias'])
    else:
        past_len = past_k.shape[1]
        k_all = jnp.concatenate([past_k.astype(k.dtype), k], axis=1)
        v_all = jnp.concatenate([past_v.astype(v.dtype), v], axis=1)
        o = sb_block(q, past_len + jnp.arange(t), k_all, v_all, jnp.arange(past_len + t), lp['sb_bias'])
    merged = (jax.nn.sigmoid(g_ssd) * (y @ lp['w_br_ssd'])
              + jax.nn.sigmoid(g_att) * (o.reshape(b, t, D_ATT) @ lp['w_br_att']))
    return merged @ lp['w_out'], conv_new, ssm_new, k, v


def conv_ffn(h, lp, prev):
    u = h @ lp['w_up']
    u, ffn_new = causal_dwconv(u, lp['ffn_conv_w'], lp['ffn_conv_b'], prev)
    val, gate = jnp.split(u, 2, axis=-1)
    return (jax.nn.silu(gate) * val) @ lp['w_down'], ffn_new


def decoder_layer(x, c, lp, conv_prev, ssm_prev, ffn_prev, past_k, past_v):
    mod = jax.nn.silu(c) @ lp['w_ada'] + lp['b_ada']
    sh1, sc1, g1, sh2, sc2, g2 = jnp.split(mod, N_MOD, axis=-1)
    h = modulate(x, lp['norm_mix'], sh1, sc1)
    mix, conv_new, ssm_new, k, v = token_mixer(h, lp, conv_prev, ssm_prev, past_k, past_v)
    x = x + g1[:, None] * mix
    h = modulate(x, lp['norm_ffn'], sh2, sc2)
    f, ffn_new = conv_ffn(h, lp, ffn_prev)
    x = x + g2[:, None] * f
    return x, k, v, ssm_new, conv_new, ffn_new


def setup_inputs(seed: int = 0) -> dict:
    key = jax.random.key(seed)
    ks = jax.random.split(key, 40)
    f32 = jnp.float32
    n_pages = PAST_LEN // PAGE_SIZE
    n_used = DEC_BATCH * n_pages
    n_pool = n_used + n_used // 4
    nrm = lambda k, shape, s: jax.random.normal(k, shape, f32) * s
    dt0 = jnp.exp(jax.random.uniform(ks[17], (DEPTH, N_SSD_HEADS), f32, math.log(1e-3), math.log(1e-1)))
    return {
        'x_prompt': nrm(ks[0], (BATCH, SEQ, D_MODEL), 1.0),
        'x_sample': nrm(ks[1], (DEC_BATCH, DEC_SEQ, D_MODEL), 1.0),
        'cache_k': nrm(ks[2], (DEPTH, n_pool, PAGE_SIZE, N_ATT_HEADS, ATT_HEAD_DIM), 1.0),
        'cache_v': nrm(ks[3], (DEPTH, n_pool, PAGE_SIZE, N_ATT_HEADS, ATT_HEAD_DIM), 1.0),
        'state_ssm': nrm(ks[4], (DEPTH, DEC_BATCH, N_SSD_HEADS, SSD_HEAD_DIM, D_STATE), 0.5),
        'state_conv': nrm(ks[5], (DEPTH, DEC_BATCH, SSD_CONV_W - 1, CONV_DIM), 1.0),
        'state_ffn_conv': nrm(ks[6], (DEPTH, DEC_BATCH, FFN_CONV_W - 1, 2 * D_FF), 1.0),
        'page_table': jax.random.permutation(ks[7], n_pool)[:n_used].reshape(DEC_BATCH, n_pages).astype(jnp.int32),
        'c_prompt': nrm(ks[8], (BATCH, D_MODEL), 1.0),
        'c_sample': nrm(ks[9], (DEC_BATCH, D_MODEL), 1.0),
        'w_ada': nrm(ks[10], (DEPTH, D_MODEL, N_MOD * D_MODEL), 0.5 * D_MODEL ** -0.5),
        'b_ada': nrm(ks[11], (DEPTH, N_MOD * D_MODEL), 0.02),
        'norm_mix': 1.0 + nrm(ks[12], (DEPTH, D_MODEL), 0.1),
        'norm_ffn': 1.0 + nrm(ks[13], (DEPTH, D_MODEL), 0.1),
        'w_in': nrm(ks[14], (DEPTH, D_MODEL, IN_COLS), D_MODEL ** -0.5),
        'conv_w': nrm(ks[15], (DEPTH, SSD_CONV_W, CONV_DIM), SSD_CONV_W ** -0.5),
        'conv_b': nrm(ks[16], (DEPTH, CONV_DIM), 0.02),
        'dt_bias': dt0 + jnp.log(-jnp.expm1(-dt0)),
        'a_log': jnp.log(jax.random.uniform(ks[18], (DEPTH, N_SSD_HEADS), f32, 1.0, 16.0)),
        'd_skip': 1.0 + nrm(ks[19], (DEPTH, N_SSD_HEADS), 0.1),
        'ssd_norm': 1.0 + nrm(ks[20], (DEPTH, D_INNER), 0.1),
        'sb_bias': SB_BIAS_INIT + nrm(ks[29], (DEPTH, N_ATT_HEADS), 0.5),
        'w_br_ssd': nrm(ks[21], (DEPTH, D_INNER, D_MODEL), D_INNER ** -0.5),
        'w_br_att': nrm(ks[22], (DEPTH, D_ATT, D_MODEL), D_ATT ** -0.5),
        'w_out': nrm(ks[23], (DEPTH, D_MODEL, D_MODEL), D_MODEL ** -0.5),
        'w_up': nrm(ks[24], (DEPTH, D_MODEL, 2 * D_FF), D_MODEL ** -0.5),
        'ffn_conv_w': nrm(ks[25], (DEPTH, FFN_CONV_W, 2 * D_FF), FFN_CONV_W ** -0.5),
        'ffn_conv_b': nrm(ks[26], (DEPTH, 2 * D_FF), 0.02),
        'w_down': nrm(ks[27], (DEPTH, D_FF, D_MODEL), D_FF ** -0.5),
        'norm_final': 1.0 + nrm(ks[28], (D_MODEL,), 0.1),
    }


def reference(x_prompt, x_sample, cache_k, cache_v, state_ssm, state_conv, state_ffn_conv, page_table,
              c_prompt, c_sample, w_ada, b_ada, norm_mix, norm_ffn, w_in, conv_w, conv_b, dt_bias, a_log,
              d_skip, ssd_norm, sb_bias, w_br_ssd, w_br_att, w_out, w_up, ffn_conv_w, ffn_conv_b, w_down,
              norm_final):
    b_p = x_prompt.shape[0]
    b_s = x_sample.shape[0]
    n_pages = page_table.shape[1]
    past_len = n_pages * cache_k.shape[2]
    conv0 = jnp.zeros((b_p, SSD_CONV_W - 1, CONV_DIM), x_prompt.dtype)
    ssm0 = jnp.zeros((b_p, N_SSD_HEADS, SSD_HEAD_DIM, D_STATE), jnp.float32)
    ffn0 = jnp.zeros((b_p, FFN_CONV_W - 1, 2 * D_FF), x_prompt.dtype)
    xp, xs = x_prompt, x_sample
    kp, vp, sp, cp, fp = [], [], [], [], []
    ksm, vsm, ssm, csm, fsm = [], [], [], [], []
    for l in range(DEPTH):
        lp = dict(w_ada=w_ada[l], b_ada=b_ada[l], norm_mix=norm_mix[l], norm_ffn=norm_ffn[l], w_in=w_in[l],
                  conv_w=conv_w[l], conv_b=conv_b[l], dt_bias=dt_bias[l], a_log=a_log[l], d_skip=d_skip[l],
                  ssd_norm=ssd_norm[l], sb_bias=sb_bias[l], w_br_ssd=w_br_ssd[l], w_br_att=w_br_att[l],
                  w_out=w_out[l], w_up=w_up[l], ffn_conv_w=ffn_conv_w[l], ffn_conv_b=ffn_conv_b[l],
                  w_down=w_down[l])
        xp, k1, v1, s1, c1, f1 = decoder_layer(xp, c_prompt, lp, conv0, ssm0, ffn0, None, None)
        kp.append(k1); vp.append(v1); sp.append(s1); cp.append(c1); fp.append(f1)
        past_k = cache_k[l][page_table].reshape(b_s, past_len, N_ATT_HEADS, ATT_HEAD_DIM)
        past_v = cache_v[l][page_table].reshape(b_s, past_len, N_ATT_HEADS, ATT_HEAD_DIM)
        xs, k2, v2, s2, c2, f2 = decoder_layer(xs, c_sample, lp, state_conv[l], state_ssm[l],
                                               state_ffn_conv[l], past_k, past_v)
        ksm.append(k2); vsm.append(v2); ssm.append(s2); csm.append(c2); fsm.append(f2)
    y_prompt = rmsnorm(xp, norm_final)
    y_sample = rmsnorm(xs, norm_final)
    k_prompt = jnp.stack(kp)
    v_prompt = jnp.stack(vp)
    ssm_prompt = jnp.stack(sp)
    conv_prompt = jnp.stack(cp)
    ffn_conv_prompt = jnp.stack(fp)
    k_sample = jnp.stack(ksm)
    v_sample = jnp.stack(vsm)
    ssm_sample = jnp.stack(ssm)
    conv_sample = jnp.stack(csm)
    ffn_conv_sample = jnp.stack(fsm)
    return (y_prompt, y_sample, k_prompt, v_prompt, ssm_prompt, conv_prompt, ffn_conv_prompt,
            k_sample, v_sample, ssm_sample, conv_sample, ffn_conv_sample)
```

```python
import functools

import jax
import jax.numpy as jnp
from jax import lax
from jax.experimental import pallas as pl
from jax.experimental.pallas import tpu as pltpu

F32 = jnp.float32
BF16 = jnp.bfloat16

D_MODEL = 1024
D_INNER = 2048
SSD_HEAD_DIM = 64
N_SSD_HEADS = 32
D_STATE = 128
N_GROUPS = 8
HEADS_PER_GROUP = N_SSD_HEADS // N_GROUPS
GROUP_W = HEADS_PER_GROUP * SSD_HEAD_DIM
SSD_CONV_W = 4
BC_DIM = N_GROUPS * D_STATE
CONV_DIM = D_INNER + 2 * BC_DIM
SSD_CHUNK = 128
ATT_HEAD_DIM = 64
N_ATT_HEADS = 16
D_ATT = 1024
D_FF = 2816
FFN_CONV_W = 3
N_MOD = 6
EPS = 1e-6
LANES = 128
VMEM_LIMIT = 56 * 1024 * 1024


def _params(*sem):
    return pltpu.CompilerParams(dimension_semantics=sem, vmem_limit_bytes=VMEM_LIMIT)


def _dot(a, b):
    return jnp.dot(a, b, preferred_element_type=F32)


def _dot_nt(a, b):
    return lax.dot_general(a, b, (((1,), (1,)), ((), ())), preferred_element_type=F32)


def _dot_tn(a, b):
    return lax.dot_general(a, b, (((0,), (0,)), ((), ())), preferred_element_type=F32)


def _silu(x):
    return x * jax.nn.sigmoid(x)


def _softplus(x):
    return jnp.maximum(x, 0.0) + jnp.log1p(jnp.exp(-jnp.abs(x)))


def _split2(x):
    hi = x.astype(BF16)
    lo = (x - hi.astype(F32)).astype(BF16)
    return hi, lo


def _split3(x):
    hi = x.astype(BF16)
    r = x - hi.astype(F32)
    mid = r.astype(BF16)
    lo = (r - mid.astype(F32)).astype(BF16)
    return hi, mid, lo


def _iota(shape, dim):
    return lax.broadcasted_iota(jnp.int32, shape, dim)


def _mod_norm(x, gain, shift, scale):
    y = x * lax.rsqrt(jnp.mean(x * x, axis=-1, keepdims=True) + EPS)
    return (y * gain) * (1.0 + scale) + shift


def _ada_kernel(c_ref, w_ref, b_ref, o_ref):
    a = _silu(c_ref[...]).astype(BF16)
    o_ref[...] = _dot(a, w_ref[...].astype(BF16)) + b_ref[...]


def _ada(c_all, w_ada, b_ada):
    depth, _, cols = w_ada.shape
    rows = c_all.shape[0]
    tn = 1536
    return pl.pallas_call(
        _ada_kernel,
        out_shape=jax.ShapeDtypeStruct((depth, rows, cols), F32),
        grid=(depth, cols // tn),
        in_specs=[pl.BlockSpec((rows, D_MODEL), lambda l, j: (0, 0)),
                  pl.BlockSpec((None, D_MODEL, tn), lambda l, j: (l, 0, j)),
                  pl.BlockSpec((None, 1, tn), lambda l, j: (l, 0, j))],
        out_specs=pl.BlockSpec((None, rows, tn), lambda l, j: (l, 0, j)),
        compiler_params=_params("arbitrary", "arbitrary"),
        name="ada",
    )(c_all, w_ada, b_ada.reshape(depth, 1, cols))


_IN_TILE = 1024
_IN_SEGMENTS = (2, 4, 1, 1, 1, 1, 1)
_IN_STARTS = (0, 2, 6, 7, 8, 9, 10)
_IN_TILES = 11


def _in_proj_kernel(x_ref, g_ref, sh_ref, sc_ref, w_ref, wdt_ref,
                    z_ref, xbc_ref, q_ref, k_ref, v_ref, gs_ref, ga_ref, dt_ref, h_ref):
    j = pl.program_id(1)

    @pl.when(j == 0)
    def _():
        h = _mod_norm(x_ref[...], g_ref[...], sh_ref[...], sc_ref[...]).astype(BF16)
        h_ref[...] = h
        dt_ref[...] = _dot(h, wdt_ref[...])

    acc = _dot(h_ref[...], w_ref[...])
    outs = (z_ref, xbc_ref, q_ref, k_ref, v_ref, gs_ref, ga_ref)
    for ref, start, count in zip(outs, _IN_STARTS, _IN_SEGMENTS):
        @pl.when((j >= start) & (j < start + count))
        def _(ref=ref):
            ref[...] = acc


def _in_proj(x, mod, tiles_per_mod, norm_g, w_main, w_dt, l, tm):
    n = x.shape[0]
    mod_rows = mod.shape[2]

    def out_spec(start, count):
        return pl.BlockSpec((tm, _IN_TILE),
                            lambda i, j: (i, jnp.clip(j - start, 0, count - 1)))

    def mod_spec(chunk):
        return pl.BlockSpec((None, None, mod_rows, D_MODEL),
                            lambda i, j: (l, i // tiles_per_mod, 0, chunk))

    widths = (D_INNER, CONV_DIM, D_ATT, D_ATT, D_ATT, D_MODEL, D_MODEL)
    out_shape = [jax.ShapeDtypeStruct((n, w), F32) for w in widths]
    out_shape.append(jax.ShapeDtypeStruct((n, LANES), F32))
    out_specs = [out_spec(s, c) for s, c in zip(_IN_STARTS, _IN_SEGMENTS)]
    out_specs.append(pl.BlockSpec((tm, LANES), lambda i, j: (i, 0)))
    return pl.pallas_call(
        _in_proj_kernel,
        out_shape=out_shape,
        grid=(n // tm, _IN_TILES),
        in_specs=[pl.BlockSpec((tm, D_MODEL), lambda i, j: (i, 0)),
                  pl.BlockSpec((None, 1, D_MODEL), lambda i, j: (l, 0, 0)),
                  mod_spec(0), mod_spec(1),
                  pl.BlockSpec((None, D_MODEL, _IN_TILE), lambda i, j: (l, 0, j)),
                  pl.BlockSpec((None, D_MODEL, LANES), lambda i, j: (l, 0, 0))],
        out_specs=out_specs,
        scratch_shapes=[pltpu.VMEM((tm, D_MODEL), BF16)],
        compiler_params=_params("arbitrary", "arbitrary"),
        name="in_proj",
    )(x, norm_g, mod, mod, w_main, w_dt)


def _up_proj_kernel(x_ref, g_ref, sh_ref, sc_ref, w_ref, u_ref, h_ref):
    @pl.when(pl.program_id(1) == 0)
    def _():
        h_ref[...] = _mod_norm(x_ref[...], g_ref[...], sh_ref[...], sc_ref[...]).astype(BF16)

    u_ref[...] = _dot(h_ref[...], w_ref[...])


def _up_proj(x, mod, tiles_per_mod, norm_g, w_up, l, tm):
    n = x.shape[0]
    mod_rows = mod.shape[2]
    tn = D_FF

    def mod_spec(chunk):
        return pl.BlockSpec((None, None, mod_rows, D_MODEL),
                            lambda i, j: (l, i // tiles_per_mod, 0, chunk))

    return pl.pallas_call(
        _up_proj_kernel,
        out_shape=jax.ShapeDtypeStruct((n, 2 * D_FF), F32),
        grid=(n // tm, 2 * D_FF // tn),
        in_specs=[pl.BlockSpec((tm, D_MODEL), lambda i, j: (i, 0)),
                  pl.BlockSpec((None, 1, D_MODEL), lambda i, j: (l, 0, 0)),
                  mod_spec(3), mod_spec(4),
                  pl.BlockSpec((None, D_MODEL, tn), lambda i, j: (l, 0, j))],
        out_specs=pl.BlockSpec((tm, tn), lambda i, j: (i, j)),
        scratch_shapes=[pltpu.VMEM((tm, D_MODEL), BF16)],
        compiler_params=_params("arbitrary", "arbitrary"),
        name="up_proj",
    )(x, norm_g, mod, mod, w_up)


def _expand_heads(v, g, width_iota_seg):
    rows = v.shape[0]
    out = jnp.broadcast_to(v[:, 4 * g + 3:4 * g + 4], (rows, GROUP_W))
    for r in (2, 1, 0):
        out = jnp.where(width_iota_seg == r, v[:, 4 * g + r:4 * g + r + 1], out)
    return out


def _ssd_prompt_kernel(xbc_ref, dt_ref, z_ref, cw_ref, cb_ref, dtb_ref, alog_ref, dskip_ref, gn_ref,
                       y_ref, ssm_ref, cbuf, xc, st, ybuf):
    c = pl.program_id(1)
    ch = SSD_CHUNK
    pad = 8

    @pl.when(c == 0)
    def _():
        cbuf[0:pad, :] = jnp.zeros((pad, CONV_DIM), F32)
        st[...] = jnp.zeros_like(st)

    cbuf[pad:pad + ch, :] = xbc_ref[...]
    cstep = 512
    for j in range(0, CONV_DIM, cstep):
        sl = slice(j, j + cstep)
        acc = cb_ref[:, sl] + cw_ref[0:1, sl] * cbuf[pad - 3:pad - 3 + ch, sl]
        for i in range(1, SSD_CONV_W):
            acc = acc + cw_ref[i:i + 1, sl] * cbuf[pad - 3 + i:pad - 3 + i + ch, sl]
        xc[:, sl] = _silu(acc)
    cbuf[0:pad, :] = cbuf[ch:ch + pad, :]

    dt = _softplus(dt_ref[...] + dtb_ref[...])
    dta = dt * (-jnp.exp(alog_ref[...]))
    ti = _iota((ch, ch), 0)
    si = _iota((ch, ch), 1)
    causal = si <= ti
    acs = jnp.dot(causal.astype(F32), dta, preferred_element_type=F32,
                  precision=lax.Precision.HIGHEST)
    acs_t = acs.T
    eacs = jnp.exp(acs)
    de = jnp.exp(acs[ch - 1:ch, :] - acs)
    dtde = dt * de
    seg = _iota((ch, GROUP_W), 1) // SSD_HEAD_DIM

    for g in range(N_GROUPS):
        b_g = xc[:, D_INNER + g * D_STATE:D_INNER + (g + 1) * D_STATE].astype(BF16)
        c_g = xc[:, D_INNER + BC_DIM + g * D_STATE:D_INNER + BC_DIM + (g + 1) * D_STATE].astype(BF16)
        cb = _dot_nt(c_g, b_g)
        xs_g = xc[:, g * GROUP_W:(g + 1) * GROUP_W]
        xdt = (xs_g * _expand_heads(dt, g, seg)).astype(BF16)
        xde = (xs_g * _expand_heads(dtde, g, seg)).astype(BF16)
        h_g = st[g * GROUP_W:(g + 1) * GROUP_W, :]
        y_g = _dot_nt(c_g, h_g.astype(BF16)) * _expand_heads(eacs, g, seg)
        for r in range(HEADS_PER_GROUP):
            h = HEADS_PER_GROUP * g + r
            diff = acs[:, h:h + 1] - acs_t[h:h + 1, :]
            m = (cb * jnp.exp(jnp.where(causal, diff, -jnp.inf))).astype(BF16)
            y_g = y_g + jnp.where(seg == r, _dot(m, xdt), 0.0)
        ybuf[:, g * GROUP_W:(g + 1) * GROUP_W] = y_g
        upd = _dot_tn(xde, b_g)
        cd = jnp.concatenate(
            [jnp.broadcast_to(eacs[ch - 1:ch, HEADS_PER_GROUP * g + r:HEADS_PER_GROUP * g + r + 1],
                              (SSD_HEAD_DIM, D_STATE)) for r in range(HEADS_PER_GROUP)], axis=0)
        st[g * GROUP_W:(g + 1) * GROUP_W, :] = h_g * cd + upd

    y = ybuf[...] + dskip_ref[...] * xc[:, 0:D_INNER]
    yg = y * _silu(z_ref[...])
    y_ref[...] = yg * lax.rsqrt(jnp.mean(yg * yg, axis=-1, keepdims=True) + EPS) * gn_ref[...]

    @pl.when(c == pl.num_programs(1) - 1)
    def _():
        ssm_ref[...] = st[...]


def _ssd_prompt(xbc, dt, z, conv_w, conv_b, dt_bias, a_log, d_skip, ssd_norm, l, batch, seq):
    ch = SSD_CHUNK
    nc = seq // ch

    def par(shape):
        return pl.BlockSpec((None,) + shape, lambda b, c: (l, 0, 0))

    return pl.pallas_call(
        _ssd_prompt_kernel,
        out_shape=[jax.ShapeDtypeStruct((batch * seq, D_INNER), F32),
                   jax.ShapeDtypeStruct((batch, N_SSD_HEADS * SSD_HEAD_DIM, D_STATE), F32)],
        grid=(batch, nc),
        in_specs=[pl.BlockSpec((ch, CONV_DIM), lambda b, c: (b * nc + c, 0)),
                  pl.BlockSpec((ch, LANES), lambda b, c: (b * nc + c, 0)),
                  pl.BlockSpec((ch, D_INNER), lambda b, c: (b * nc + c, 0)),
                  par((SSD_CONV_W, CONV_DIM)), par((1, CONV_DIM)), par((1, LANES)), par((1, LANES)),
                  par((1, D_INNER)), par((1, D_INNER))],
        out_specs=[pl.BlockSpec((ch, D_INNER), lambda b, c: (b * nc + c, 0)),
                   pl.BlockSpec((None, N_SSD_HEADS * SSD_HEAD_DIM, D_STATE), lambda b, c: (b, 0, 0))],
        scratch_shapes=[pltpu.VMEM((ch + 8, CONV_DIM), F32),
                        pltpu.VMEM((ch, CONV_DIM), F32),
                        pltpu.VMEM((N_SSD_HEADS * SSD_HEAD_DIM, D_STATE), F32),
                        pltpu.VMEM((ch, D_INNER), F32)],
        compiler_params=_params("arbitrary", "arbitrary"),
        name="ssd_prompt",
    )(xbc, dt, z, conv_w, conv_b, dt_bias, a_log, d_skip, ssd_norm)


_QB = 128


def _sb_prompt_kernel(q_ref, k_ref, v_ref, b_ref, o_ref):
    qi = pl.program_id(2)
    blk = _QB
    lane = _iota((blk, LANES), 1)
    row = _iota((blk, blk), 0)
    col = _iota((blk, blk), 1)
    strictly_lower = col < row
    first_head = lane < ATT_HEAD_DIM
    jj = _iota((blk, 2 * blk), 0)
    ss = _iota((blk, 2 * blk), 1)
    tri = ((jj > ss) | (ss >= blk)).astype(BF16)
    q2 = q_ref[...] * (ATT_HEAD_DIM ** -0.5)

    def block(kj, carry, acc, qm, bias, masked):
        start = pl.multiple_of(kj * blk, blk)
        kb = k_ref[pl.ds(start, blk), :].astype(BF16)
        vb = v_ref[pl.ds(start, blk), :].astype(BF16)
        z = _dot_nt(qm, kb) + bias
        sp = _softplus(z)
        if masked:
            sp = jnp.where(strictly_lower, sp, 0.0)
        hi, lo = _split2(sp)
        s2 = _dot(hi, tri) + _dot(lo, tri)
        w = jnp.exp(z - sp - (s2[:, 0:blk] + carry))
        if masked:
            w = jnp.where(strictly_lower, w, 0.0)
        acc = acc + _dot(w.astype(BF16), vb)
        return carry + s2[:, blk:2 * blk], acc

    outs = []
    for hh in range(2):
        sel = first_head if hh == 0 else jnp.logical_not(first_head)
        qm = jnp.where(sel, q2, 0.0).astype(BF16)
        bias = b_ref[hh:hh + 1, :]
        zero = jnp.zeros((blk, blk), F32)
        carry, acc = block(qi, zero, zero, qm, bias, True)

        def body(t, state, qm=qm, bias=bias):
            return block(qi - 1 - t, state[0], state[1], qm, bias, False)

        carry, acc = lax.fori_loop(0, qi, body, (carry, acc))
        outs.append(acc)
    o_ref[...] = jnp.where(first_head, outs[0], outs[1])


def _sb_prompt(q, k, v, bias2, l, batch, seq):
    nq = seq // _QB
    pairs = N_ATT_HEADS // 2
    return pl.pallas_call(
        _sb_prompt_kernel,
        out_shape=jax.ShapeDtypeStruct((batch * seq, D_ATT), F32),
        grid=(batch, pairs, nq),
        in_specs=[pl.BlockSpec((_QB, LANES), lambda b, h, i: (b * nq + i, h)),
                  pl.BlockSpec((seq, LANES), lambda b, h, i: (b, h)),
                  pl.BlockSpec((seq, LANES), lambda b, h, i: (b, h)),
                  pl.BlockSpec((None, None, 2, LANES), lambda b, h, i: (l, h, 0, 0))],
        out_specs=pl.BlockSpec((_QB, LANES), lambda b, h, i: (b * nq + i, h)),
        compiler_params=_params("arbitrary", "arbitrary", "arbitrary"),
        name="sb_prompt",
    )(q, k, v, bias2)


def _sb_sample_kernel(pt_ref, q_ref, kn_ref, vn_ref, kc_ref, vc_ref, brow_ref, o_ref,
                      qrt_ref, kpad_ref, vpad_ref, carry_ref, acc_ref):
    del pt_ref
    p = pl.program_id(1)
    t_new = q_ref.shape[0]
    page = kc_ref.shape[0]
    ncol = t_new * N_ATT_HEADS
    jj = _iota((page, page), 1)
    ss = _iota((page, page), 0)
    later = (jj > ss).astype(BF16)

    def process(kf, vf, mask):
        z = _dot(kf.astype(BF16), qrt_ref[...]) + brow_ref[...]
        sp = _softplus(z)
        if mask is not None:
            sp = jnp.where(mask, sp, 0.0)
        hi, lo = _split2(sp)
        suffix = _dot(later, hi) + _dot(later, lo) + carry_ref[...]
        w = jnp.exp(z - sp - suffix)
        if mask is not None:
            w = jnp.where(mask, w, 0.0)
        wt = w.T[0:ncol, :].astype(BF16)
        acc_ref[...] += _dot(wt, vf.astype(BF16))
        carry_ref[...] += jnp.sum(sp, axis=0, keepdims=True)

    @pl.when(p == 0)
    def _():
        q = q_ref[...] * (ATT_HEAD_DIM ** -0.5)
        rows = jnp.concatenate(
            [jnp.broadcast_to(q[t:t + 1, :], (N_ATT_HEADS, D_ATT)) for t in range(t_new)]
            + [jnp.zeros((LANES - ncol, D_ATT), F32)], axis=0)
        r_i = _iota((LANES, D_ATT), 0)
        l_i = _iota((LANES, D_ATT), 1)
        own = (l_i // ATT_HEAD_DIM) == (r_i % N_ATT_HEADS)
        qrt_ref[...] = jnp.where(own, rows, 0.0).T.astype(BF16)
        carry_ref[...] = jnp.zeros_like(carry_ref)
        acc_ref[...] = jnp.zeros_like(acc_ref)
        kpad_ref[...] = jnp.zeros_like(kpad_ref)
        vpad_ref[...] = jnp.zeros_like(vpad_ref)
        kpad_ref[0:t_new, :] = kn_ref[...]
        vpad_ref[0:t_new, :] = vn_ref[...]
        s_i = _iota((page, LANES), 0)
        c_i = _iota((page, LANES), 1)
        mask = s_i < jnp.minimum(c_i // N_ATT_HEADS, t_new)
        process(kpad_ref[...], vpad_ref[...], mask)

    @pl.when(p > 0)
    def _():
        process(kc_ref[...], vc_ref[...], None)

    @pl.when(p == pl.num_programs(1) - 1)
    def _():
        r_i = _iota((ncol, D_ATT), 0)
        l_i = _iota((ncol, D_ATT), 1)
        own = (l_i // ATT_HEAD_DIM) == (r_i % N_ATT_HEADS)
        picked = jnp.where(own, acc_ref[...], 0.0)
        for t in range(t_new):
            o_ref[t:t + 1, :] = jnp.sum(picked[t * N_ATT_HEADS:(t + 1) * N_ATT_HEADS, :],
                                        axis=0, keepdims=True)


def _sb_sample(page_table, q_bm, k_bm, v_bm, cache_k, cache_v, brow, l):
    nb, t_new, _ = q_bm.shape
    n_pages = page_table.shape[1]
    page = cache_k.shape[2]
    ncol = t_new * N_ATT_HEADS

    def cache_map(b, p, pt):
        return (l, pt[b, jnp.minimum(n_pages - p, n_pages - 1)], 0, 0)

    tok = pl.BlockSpec((None, t_new, D_ATT), lambda b, p, pt: (b, 0, 0))
    grid_spec = pltpu.PrefetchScalarGridSpec(
        num_scalar_prefetch=1,
        grid=(nb, n_pages + 1),
        in_specs=[tok, tok, tok,
                  pl.BlockSpec((None, None, page, D_ATT), cache_map),
                  pl.BlockSpec((None, None, page, D_ATT), cache_map),
                  pl.BlockSpec((None, 1, LANES), lambda b, p, pt: (l, 0, 0))],
        out_specs=tok,
        scratch_shapes=[pltpu.VMEM((D_ATT, LANES), BF16),
                        pltpu.VMEM((page, D_ATT), F32),
                        pltpu.VMEM((page, D_ATT), F32),
                        pltpu.VMEM((1, LANES), F32),
                        pltpu.VMEM((ncol, D_ATT), F32)])
    return pl.pallas_call(
        _sb_sample_kernel,
        out_shape=jax.ShapeDtypeStruct((nb, t_new, D_ATT), F32),
        grid_spec=grid_spec,
        compiler_params=_params("arbitrary", "arbitrary"),
        name="sb_sample",
    )(page_table, q_bm, k_bm, v_bm, cache_k, cache_v, brow)


def _ssd_sample_pre_kernel(xbc_ref, prev_ref, dt_ref, cw_ref, cb_ref, dtb_ref, alog_ref, seg_ref, exp_ref,
                           xc_ref, yd_ref, xd_ref, ea_ref, cd_ref):
    t_new = xbc_ref.shape[0]
    taps = SSD_CONV_W

    def tap(r, sl):
        return prev_ref[r, :, sl] if r < taps - 1 else xbc_ref[r - (taps - 1), :, sl]

    cstep = 512
    for t in range(t_new):
        for j in range(0, CONV_DIM, cstep):
            sl = slice(j, j + cstep)
            acc = cb_ref[:, sl] + cw_ref[0:1, sl] * tap(t, sl)
            for i in range(1, taps):
                acc = acc + cw_ref[i:i + 1, sl] * tap(t + i, sl)
            xc_ref[t, :, sl] = _silu(acc)

    a = -jnp.exp(alog_ref[...])
    dt = [_softplus(dt_ref[t] + dtb_ref[...]) for t in range(t_new)]
    acs = []
    for t in range(t_new):
        step = dt[t] * a
        acs.append(step if t == 0 else acs[t - 1] + step)

    def expand(v):
        e = exp_ref[...]
        hi, mid, lo = _split3(v)
        return _dot(hi, e) + _dot(mid, e) + _dot(lo, e)

    seg = seg_ref[...]
    for t in range(t_new):
        c_t = xc_ref[t, :, D_INNER + BC_DIM:CONV_DIM]
        yd = None
        for s in range(t + 1):
            b_s = xc_ref[s, :, D_INNER:D_INNER + BC_DIM]
            hi, lo = _split2(c_t * b_s)
            cbh = _dot(hi, seg) + _dot(lo, seg)
            coef = cbh * jnp.exp(acs[t] - acs[s]) * dt[s]
            term = expand(coef) * xc_ref[s, :, 0:D_INNER]
            yd = term if yd is None else yd + term
        yd_ref[t] = yd
        ea_ref[t] = expand(jnp.exp(acs[t]))
        xd_ref[t] = expand(dt[t] * jnp.exp(acs[t_new - 1] - acs[t])) * xc_ref[t, :, 0:D_INNER]
    cd_ref[...] = jnp.exp(acs[t_new - 1])


def _ssd_sample_pre(xbc_tm, prev_tm, dt_tm, conv_w, conv_b, dt_bias, a_log, seg_mat, exp_mat, l):
    t_new, nb, _ = xbc_tm.shape
    bt = 32

    def par(shape):
        return pl.BlockSpec((None,) + shape, lambda i: (l, 0, 0))

    def tm_spec(t, w):
        return pl.BlockSpec((t, bt, w), lambda i: (0, i, 0))

    return pl.pallas_call(
        _ssd_sample_pre_kernel,
        out_shape=[jax.ShapeDtypeStruct((t_new, nb, CONV_DIM), F32),
                   jax.ShapeDtypeStruct((t_new, nb, D_INNER), F32),
                   jax.ShapeDtypeStruct((t_new, nb, D_INNER), F32),
                   jax.ShapeDtypeStruct((t_new, nb, D_INNER), F32),
                   jax.ShapeDtypeStruct((nb, LANES), F32)],
        grid=(nb // bt,),
        in_specs=[tm_spec(t_new, CONV_DIM), tm_spec(SSD_CONV_W - 1, CONV_DIM), tm_spec(t_new, LANES),
                  par((SSD_CONV_W, CONV_DIM)), par((1, CONV_DIM)), par((1, LANES)), par((1, LANES)),
                  pl.BlockSpec((BC_DIM, LANES), lambda i: (0, 0)),
                  pl.BlockSpec((LANES, D_INNER), lambda i: (0, 0))],
        out_specs=[tm_spec(t_new, CONV_DIM), tm_spec(t_new, D_INNER), tm_spec(t_new, D_INNER),
                   tm_spec(t_new, D_INNER), pl.BlockSpec((bt, LANES), lambda i: (i, 0))],
        compiler_params=_params("arbitrary"),
        name="ssd_sample_pre",
    )(xbc_tm, prev_tm, dt_tm, conv_w, conv_b, dt_bias, a_log, seg_mat, exp_mat)


_TPAD = 16


def _ssd_sample_state_kernel(st_ref, bc_ref, xd_ref, cd_ref, new_ref, yoff_ref):
    bb = st_ref.shape[0]
    for jb in range(bb):
        bc = bc_ref[jb]
        xd = xd_ref[jb].astype(BF16)
        cdb = cd_ref[jb]
        for g in range(N_GROUPS):
            rows = slice(g * GROUP_W, (g + 1) * GROUP_W)
            h_g = st_ref[jb, rows, :]
            b_g = bc[:, g * D_STATE:(g + 1) * D_STATE].astype(BF16)
            c_g = bc[:, BC_DIM + g * D_STATE:BC_DIM + (g + 1) * D_STATE].astype(BF16)
            yoff_ref[jb, :, rows] = _dot_nt(c_g, h_g.astype(BF16))
            upd = _dot_tn(xd[:, rows], b_g)
            cd = jnp.concatenate(
                [jnp.broadcast_to(cdb[0:1, HEADS_PER_GROUP * g + r:HEADS_PER_GROUP * g + r + 1],
                                  (SSD_HEAD_DIM, D_STATE)) for r in range(HEADS_PER_GROUP)], axis=0)
            new_ref[jb, rows, :] = h_g * cd + upd


def _ssd_sample_state(state_ssm, bc_bm, xd_bm, cd, l):
    nb = bc_bm.shape[0]
    bb = 4
    rows = N_SSD_HEADS * SSD_HEAD_DIM
    return pl.pallas_call(
        _ssd_sample_state_kernel,
        out_shape=[jax.ShapeDtypeStruct((nb, rows, D_STATE), F32),
                   jax.ShapeDtypeStruct((nb, _TPAD, D_INNER), F32)],
        grid=(nb // bb,),
        in_specs=[pl.BlockSpec((None, bb, rows, D_STATE), lambda i: (l, i, 0, 0)),
                  pl.BlockSpec((bb, _TPAD, 2 * BC_DIM), lambda i: (i, 0, 0)),
                  pl.BlockSpec((bb, _TPAD, D_INNER), lambda i: (i, 0, 0)),
                  pl.BlockSpec((bb, 1, LANES), lambda i: (i, 0, 0))],
        out_specs=[pl.BlockSpec((bb, rows, D_STATE), lambda i: (i, 0, 0)),
                   pl.BlockSpec((bb, _TPAD, D_INNER), lambda i: (i, 0, 0))],
        compiler_params=_params("arbitrary"),
        name="ssd_sample_state",
    )(state_ssm, bc_bm, xd_bm, cd)


def _ssd_sample_post_kernel(yd_ref, ea_ref, yoff_ref, xs_ref, z_ref, dskip_ref, gn_ref, y_ref):
    y = yd_ref[...] + ea_ref[...] * yoff_ref[...] + dskip_ref[...] * xs_ref[...]
    yg = y * _silu(z_ref[...])
    y_ref[...] = yg * lax.rsqrt(jnp.mean(yg * yg, axis=-1, keepdims=True) + EPS) * gn_ref[...]


def _ssd_sample_post(yd, ea, yoff, xc, z, d_skip, ssd_norm, l):
    n = yd.shape[0]
    tm = 128
    row = pl.BlockSpec((tm, D_INNER), lambda i: (i, 0))
    par = pl.BlockSpec((None, 1, D_INNER), lambda i: (l, 0, 0))
    return pl.pallas_call(
        _ssd_sample_post_kernel,
        out_shape=jax.ShapeDtypeStruct((n, D_INNER), F32),
        grid=(n // tm,),
        in_specs=[row, row, row, row, row, par, par],
        out_specs=row,
        compiler_params=_params("arbitrary"),
        name="ssd_sample_post",
    )(yd, ea, yoff, xc, z, d_skip, ssd_norm)


def _mix_kernel(y_ref, o_ref, gs_ref, ga_ref, x_ref, g1_ref, wbs_ref, wba_ref, wo_ref, out_ref):
    a = _dot(y_ref[...].astype(BF16), wbs_ref[...])
    b = _dot(o_ref[...].astype(BF16), wba_ref[...])
    merged = jax.nn.sigmoid(gs_ref[...]) * a + jax.nn.sigmoid(ga_ref[...]) * b
    mix = _dot(merged.astype(BF16), wo_ref[...])
    out_ref[...] = x_ref[...] + g1_ref[...] * mix


def _mix(y, o, gs, ga, x, mod, tiles_per_mod, w_br_ssd, w_br_att, w_out, l, tm):
    n = x.shape[0]
    mod_rows = mod.shape[2]

    def row(w):
        return pl.BlockSpec((tm, w), lambda i: (i, 0))

    def wspec(k):
        return pl.BlockSpec((None, k, D_MODEL), lambda i: (l, 0, 0))

    return pl.pallas_call(
        _mix_kernel,
        out_shape=jax.ShapeDtypeStruct((n, D_MODEL), F32),
        grid=(n // tm,),
        in_specs=[row(D_INNER), row(D_ATT), row(D_MODEL), row(D_MODEL), row(D_MODEL),
                  pl.BlockSpec((None, None, mod_rows, D_MODEL), lambda i: (l, i // tiles_per_mod, 0, 2)),
                  wspec(D_INNER), wspec(D_ATT), wspec(D_MODEL)],
        out_specs=row(D_MODEL),
        compiler_params=_params("arbitrary"),
        name="mix",
    )(y, o, gs, ga, x, mod, w_br_ssd, w_br_att, w_out)


_FFN_STEP = 256


def _ffn_gate(tap_rows, cw_ref, cb_ref, act_ref):
    for j in range(0, D_FF, _FFN_STEP):
        halves = []
        for off in (0, D_FF):
            sl = slice(off + j, off + j + _FFN_STEP)
            acc = cb_ref[:, sl] + cw_ref[0:1, sl] * tap_rows(0, sl)
            for i in range(1, FFN_CONV_W):
                acc = acc + cw_ref[i:i + 1, sl] * tap_rows(i, sl)
            halves.append(acc)
        val, gate = halves
        act_ref[:, j:j + _FFN_STEP] = (_silu(gate) * val).astype(BF16)


def _ffn_prompt_kernel(u_ref, x_ref, g2_ref, cw_ref, cb_ref, wd_ref, out_ref, cbuf, act):
    tm = u_ref.shape[0]
    pad = 8

    @pl.when(pl.program_id(1) == 0)
    def _():
        cbuf[0:pad, :] = jnp.zeros((pad, 2 * D_FF), F32)

    cbuf[pad:pad + tm, :] = u_ref[...]
    first = pad - (FFN_CONV_W - 1)
    _ffn_gate(lambda i, sl: cbuf[first + i:first + i + tm, sl], cw_ref, cb_ref, act)
    cbuf[0:pad, :] = cbuf[tm:tm + pad, :]
    out_ref[...] = x_ref[...] + g2_ref[...] * _dot(act[...], wd_ref[...])


def _ffn_prompt(u, x, mod, ffn_conv_w, ffn_conv_b, w_down, l, batch, seq, tm):
    nt = seq // tm

    def par(shape):
        return pl.BlockSpec((None,) + shape, lambda b, c: (l, 0, 0))

    return pl.pallas_call(
        _ffn_prompt_kernel,
        out_shape=jax.ShapeDtypeStruct((batch * seq, D_MODEL), F32),
        grid=(batch, nt),
        in_specs=[pl.BlockSpec((tm, 2 * D_FF), lambda b, c: (b * nt + c, 0)),
                  pl.BlockSpec((tm, D_MODEL), lambda b, c: (b * nt + c, 0)),
                  pl.BlockSpec((None, None, 1, D_MODEL), lambda b, c: (l, b, 0, 5)),
                  par((FFN_CONV_W, 2 * D_FF)), par((1, 2 * D_FF)), par((D_FF, D_MODEL))],
        out_specs=pl.BlockSpec((tm, D_MODEL), lambda b, c: (b * nt + c, 0)),
        scratch_shapes=[pltpu.VMEM((tm + 8, 2 * D_FF), F32), pltpu.VMEM((tm, D_FF), BF16)],
        compiler_params=_params("arbitrary", "arbitrary"),
        name="ffn_prompt",
    )(u, x, mod, ffn_conv_w, ffn_conv_b, w_down)


def _ffn_sample_kernel(u_ref, prev_ref, x_ref, g2_ref, cw_ref, cb_ref, wd_ref, out_ref, act):
    t_new, nb, _ = u_ref.shape
    n_prev = FFN_CONV_W - 1
    for t in range(t_new):
        def tap_rows(i, sl, t=t):
            src = t + i - n_prev
            return u_ref[src, :, sl] if src >= 0 else prev_ref[src + n_prev, :, sl]

        _ffn_gate(tap_rows, cw_ref, cb_ref, act.at[t * nb:(t + 1) * nb])
    out_ref[...] = x_ref[...] + g2_ref[...] * _dot(act[...], wd_ref[...])


def _ffn_sample(u_tm, prev_tm, x, mod, ffn_conv_w, ffn_conv_b, w_down, l):
    t_new, nb, _ = u_tm.shape
    n = t_new * nb

    def par(shape):
        return pl.BlockSpec((None,) + shape, lambda i: (l, 0, 0))

    return pl.pallas_call(
        _ffn_sample_kernel,
        out_shape=jax.ShapeDtypeStruct((n, D_MODEL), F32),
        grid=(1,),
        in_specs=[pl.BlockSpec((t_new, nb, 2 * D_FF), lambda i: (0, 0, 0)),
                  pl.BlockSpec((FFN_CONV_W - 1, nb, 2 * D_FF), lambda i: (0, 0, 0)),
                  pl.BlockSpec((n, D_MODEL), lambda i: (0, 0)),
                  pl.BlockSpec((None, None, n, D_MODEL), lambda i: (l, 0, 0, 5)),
                  par((FFN_CONV_W, 2 * D_FF)), par((1, 2 * D_FF)), par((D_FF, D_MODEL))],
        out_specs=pl.BlockSpec((n, D_MODEL), lambda i: (0, 0)),
        scratch_shapes=[pltpu.VMEM((n, D_FF), BF16)],
        compiler_params=_params("arbitrary"),
        name="ffn_sample",
    )(u_tm, prev_tm, x, mod, ffn_conv_w, ffn_conv_b, w_down)


def _final_norm_kernel(x_ref, g_ref, o_ref):
    x = x_ref[...]
    o_ref[...] = x * lax.rsqrt(jnp.mean(x * x, axis=-1, keepdims=True) + EPS) * g_ref[...]


def _final_norm(x, g, tm):
    n = x.shape[0]
    return pl.pallas_call(
        _final_norm_kernel,
        out_shape=jax.ShapeDtypeStruct((n, D_MODEL), F32),
        grid=(n // tm,),
        in_specs=[pl.BlockSpec((tm, D_MODEL), lambda i: (i, 0)),
                  pl.BlockSpec((1, D_MODEL), lambda i: (0, 0))],
        out_specs=pl.BlockSpec((tm, D_MODEL), lambda i: (i, 0)),
        compiler_params=_params("arbitrary"),
        name="final_norm",
    )(x, g.reshape(1, D_MODEL))


def _to_batch_major(a_tm, t_new, nb):
    return a_tm.reshape(t_new, nb, -1).transpose(1, 0, 2)


def _pad_time(a_bm):
    return jnp.pad(a_bm, ((0, 0), (0, _TPAD - a_bm.shape[1]), (0, 0)))


def kernel(x_prompt, x_sample, cache_k, cache_v, state_ssm, state_conv, state_ffn_conv, page_table,
           c_prompt, c_sample, w_ada, b_ada, norm_mix, norm_ffn, w_in, conv_w, conv_b, dt_bias, a_log,
           d_skip, ssd_norm, sb_bias, w_br_ssd, w_br_att, w_out, w_up, ffn_conv_w, ffn_conv_b, w_down,
           norm_final):
    batch, seq, _ = x_prompt.shape
    nb, t_new, _ = x_sample.shape
    depth = w_in.shape[0]
    n_s = nb * t_new
    tm_p = 512

    mods = _ada(jnp.concatenate([c_prompt, c_sample], axis=0), w_ada, b_ada)
    mod_p = mods[:, :batch].reshape(depth, batch, 1, N_MOD * D_MODEL)
    mod_s = jnp.tile(mods[:, batch:], (1, t_new, 1)).reshape(depth, 1, n_s, N_MOD * D_MODEL)

    cuts = (D_INNER, D_INNER + CONV_DIM)
    w_main = jnp.concatenate([w_in[:, :, :cuts[1]], w_in[:, :, cuts[1] + N_SSD_HEADS:]], axis=2).astype(BF16)
    w_dt = jnp.pad(w_in[:, :, cuts[1]:cuts[1] + N_SSD_HEADS],
                   ((0, 0), (0, 0), (0, LANES - N_SSD_HEADS))).astype(BF16)
    w_br_ssd_b = w_br_ssd.astype(BF16)
    w_br_att_b = w_br_att.astype(BF16)
    w_out_b = w_out.astype(BF16)
    w_up_b = w_up.astype(BF16)
    w_down_b = w_down.astype(BF16)

    def row3(a):
        return a.reshape(depth, 1, a.shape[-1])

    def head_row(a):
        return jnp.pad(a, ((0, 0), (0, LANES - N_SSD_HEADS))).reshape(depth, 1, LANES)

    norm_mix3, norm_ffn3 = row3(norm_mix), row3(norm_ffn)
    conv_b3, ffn_conv_b3, ssd_norm3 = row3(conv_b), row3(ffn_conv_b), row3(ssd_norm)
    dt_bias3, a_log3 = head_row(dt_bias), head_row(a_log)
    d_skip3 = row3(jnp.repeat(d_skip, SSD_HEAD_DIM, axis=1))
    bias2 = jnp.broadcast_to(sb_bias[:, :, None], (depth, N_ATT_HEADS, LANES)).reshape(
        depth, N_ATT_HEADS // 2, 2, LANES)
    brow = jnp.pad(jnp.tile(sb_bias, (1, t_new)),
                   ((0, 0), (0, LANES - t_new * N_ATT_HEADS))).reshape(depth, 1, LANES)
    seg_mat = (jnp.arange(BC_DIM)[:, None] // D_STATE
               == jnp.where(jnp.arange(LANES) < N_SSD_HEADS, jnp.arange(LANES) // HEADS_PER_GROUP, -1)[None, :]
               ).astype(BF16)
    exp_mat = (jnp.arange(LANES)[:, None] == jnp.arange(D_INNER)[None, :] // SSD_HEAD_DIM).astype(BF16)

    cache_k4 = cache_k.reshape(cache_k.shape[:3] + (D_ATT,))
    cache_v4 = cache_v.reshape(cache_v.shape[:3] + (D_ATT,))
    state_ssm4 = state_ssm.reshape(depth, nb, N_SSD_HEADS * SSD_HEAD_DIM, D_STATE)

    xp = x_prompt.reshape(batch * seq, D_MODEL)
    xs = x_sample.transpose(1, 0, 2).reshape(n_s, D_MODEL)
    outs = {name: [] for name in ("kp", "vp", "sp", "cp", "fp", "ks", "vs", "ss", "cs", "fs")}

    for l in range(depth):
        z, xbc, q, k, v, gs, ga, dt = _in_proj(xp, mod_p, seq // tm_p, norm_mix3, w_main, w_dt, l, tm_p)
        y, ssm_new = _ssd_prompt(xbc, dt, z, conv_w, conv_b3, dt_bias3, a_log3, d_skip3, ssd_norm3,
                                 l, batch, seq)
        o = _sb_prompt(q, k, v, bias2, l, batch, seq)
        xp = _mix(y, o, gs, ga, xp, mod_p, seq // 256, w_br_ssd_b, w_br_att_b, w_out_b, l, 256)
        u = _up_proj(xp, mod_p, seq // tm_p, norm_ffn3, w_up_b, l, tm_p)
        xp = _ffn_prompt(u, xp, mod_p, ffn_conv_w, ffn_conv_b3, w_down_b, l, batch, seq, 256)
        outs["kp"].append(k.reshape(batch, seq, N_ATT_HEADS, ATT_HEAD_DIM))
        outs["vp"].append(v.reshape(batch, seq, N_ATT_HEADS, ATT_HEAD_DIM))
        outs["sp"].append(ssm_new.reshape(batch, N_SSD_HEADS, SSD_HEAD_DIM, D_STATE))
        outs["cp"].append(xbc.reshape(batch, seq, CONV_DIM)[:, seq - (SSD_CONV_W - 1):])
        outs["fp"].append(u.reshape(batch, seq, 2 * D_FF)[:, seq - (FFN_CONV_W - 1):])

        z, xbc, q, k, v, gs, ga, dt = _in_proj(xs, mod_s, 1, norm_mix3, w_main, w_dt, l, n_s)
        xbc_tm = xbc.reshape(t_new, nb, CONV_DIM)
        xc, yd, xd, ea, cd = _ssd_sample_pre(
            xbc_tm, state_conv[l].transpose(1, 0, 2), dt.reshape(t_new, nb, LANES),
            conv_w, conv_b3, dt_bias3, a_log3, seg_mat, exp_mat, l)
        bc_bm = _pad_time(xc[:, :, D_INNER:].transpose(1, 0, 2))
        xd_bm = _pad_time(xd.transpose(1, 0, 2))
        ssm_new, yoff_bm = _ssd_sample_state(state_ssm4, bc_bm, xd_bm, cd.reshape(nb, 1, LANES), l)
        yoff = yoff_bm[:, :t_new].transpose(1, 0, 2).reshape(n_s, D_INNER)
        y = _ssd_sample_post(yd.reshape(n_s, D_INNER), ea.reshape(n_s, D_INNER), yoff,
                             xc.reshape(n_s, CONV_DIM), z, d_skip3, ssd_norm3, l)
        q_bm, k_bm, v_bm = (_to_batch_major(a, t_new, nb) for a in (q, k, v))
        o_bm = _sb_sample(page_table, q_bm, k_bm, v_bm, cache_k4, cache_v4, brow, l)
        o = o_bm.transpose(1, 0, 2).reshape(n_s, D_ATT)
        xs = _mix(y, o, gs, ga, xs, mod_s, 1, w_br_ssd_b, w_br_att_b, w_out_b, l, n_s)
        u = _up_proj(xs, mod_s, 1, norm_ffn3, w_up_b, l, n_s)
        u_tm = u.reshape(t_new, nb, 2 * D_FF)
        xs = _ffn_sample(u_tm, state_ffn_conv[l].transpose(1, 0, 2), xs, mod_s,
                         ffn_conv_w, ffn_conv_b3, w_down_b, l)
        outs["ks"].append(k_bm.reshape(nb, t_new, N_ATT_HEADS, ATT_HEAD_DIM))
        outs["vs"].append(v_bm.reshape(nb, t_new, N_ATT_HEADS, ATT_HEAD_DIM))
        outs["ss"].append(ssm_new.reshape(nb, N_SSD_HEADS, SSD_HEAD_DIM, D_STATE))
        outs["cs"].append(xbc_tm[t_new - (SSD_CONV_W - 1):].transpose(1, 0, 2))
        outs["fs"].append(u_tm[t_new - (FFN_CONV_W - 1):].transpose(1, 0, 2))

    y_prompt = _final_norm(xp, norm_final, 512).reshape(batch, seq, D_MODEL)
    y_sample = _final_norm(xs, norm_final, n_s).reshape(t_new, nb, D_MODEL).transpose(1, 0, 2)
    st = {name: jnp.stack(vals) for name, vals in outs.items()}
    return (y_prompt, y_sample, st["kp"], st["vp"], st["sp"], st["cp"], st["fp"],
            st["ks"], st["vs"], st["ss"], st["cs"], st["fs"])
```

```python
import functools

import jax
import jax.numpy as jnp
from jax import lax
from jax.experimental import pallas as pl
from jax.experimental.pallas import tpu as pltpu

F32 = jnp.float32
BF16 = jnp.bfloat16

D_MODEL = 1024
D_INNER = 2048
SSD_HEAD_DIM = 64
N_SSD_HEADS = 32
D_STATE = 128
N_GROUPS = 8
HEADS_PER_GROUP = N_SSD_HEADS // N_GROUPS
GROUP_W = HEADS_PER_GROUP * SSD_HEAD_DIM
SSD_CONV_W = 4
BC_DIM = N_GROUPS * D_STATE
CONV_DIM = D_INNER + 2 * BC_DIM
SSD_CHUNK = 128
ATT_HEAD_DIM = 64
N_ATT_HEADS = 16
D_ATT = 1024
D_FF = 2816
FFN_CONV_W = 3
N_MOD = 6
EPS = 1e-6
LANES = 128
VMEM_LIMIT = 56 * 1024 * 1024


def _params(*sem):
    return pltpu.CompilerParams(dimension_semantics=sem, vmem_limit_bytes=VMEM_LIMIT)


def _dot(a, b):
    return jnp.dot(a, b, preferred_element_type=F32)


def _dot_nt(a, b):
    return lax.dot_general(a, b, (((1,), (1,)), ((), ())), preferred_element_type=F32)


def _dot_tn(a, b):
    return lax.dot_general(a, b, (((0,), (0,)), ((), ())), preferred_element_type=F32)


def _silu(x):
    return x * jax.nn.sigmoid(x)


def _softplus(x):
    return jnp.maximum(x, 0.0) + jnp.log1p(jnp.exp(-jnp.abs(x)))


def _split2(x):
    hi = x.astype(BF16)
    lo = (x - hi.astype(F32)).astype(BF16)
    return hi, lo


def _split3(x):
    hi = x.astype(BF16)
    r = x - hi.astype(F32)
    mid = r.astype(BF16)
    lo = (r - mid.astype(F32)).astype(BF16)
    return hi, mid, lo


def _iota(shape, dim):
    return lax.broadcasted_iota(jnp.int32, shape, dim)


def _mod_norm(x, gain, shift, scale):
    y = x * lax.rsqrt(jnp.mean(x * x, axis=-1, keepdims=True) + EPS)
    return (y * gain) * (1.0 + scale) + shift


def _ada_kernel(c_ref, w_ref, b_ref, o_ref):
    a = _silu(c_ref[...]).astype(BF16)
    o_ref[...] = _dot(a, w_ref[...].astype(BF16)) + b_ref[...]


def _ada(c_all, w_ada, b_ada):
    depth, _, cols = w_ada.shape
    rows = c_all.shape[0]
    tn = 1536
    return pl.pallas_call(
        _ada_kernel,
        out_shape=jax.ShapeDtypeStruct((depth, rows, cols), F32),
        grid=(depth, cols // tn),
        in_specs=[pl.BlockSpec((rows, D_MODEL), lambda l, j: (0, 0)),
                  pl.BlockSpec((None, D_MODEL, tn), lambda l, j: (l, 0, j)),
                  pl.BlockSpec((None, 1, tn), lambda l, j: (l, 0, j))],
        out_specs=pl.BlockSpec((None, rows, tn), lambda l, j: (l, 0, j)),
        compiler_params=_params("arbitrary", "arbitrary"),
        name="ada",
    )(c_all, w_ada, b_ada.reshape(depth, 1, cols))


_IN_TILE = 1024
PROJ_COLS = CONV_DIM + D_INNER + 3 * D_ATT + 2 * D_MODEL
COL_XBC = 0
COL_Z = COL_XBC + CONV_DIM
COL_Q = COL_Z + D_INNER
COL_K = COL_Q + D_ATT
COL_V = COL_K + D_ATT
COL_GS = COL_V + D_ATT
COL_GA = COL_GS + D_MODEL


def _in_proj_kernel(x_ref, g_ref, sh_ref, sc_ref, w_ref, wdt_ref, proj_ref, dt_ref, h_ref):
    @pl.when(pl.program_id(1) == 0)
    def _():
        h = _mod_norm(x_ref[...], g_ref[...], sh_ref[...], sc_ref[...]).astype(BF16)
        h_ref[...] = h
        dt_ref[...] = _dot(h, wdt_ref[...])

    proj_ref[...] = _dot(h_ref[...], w_ref[...])


def _in_proj(x, mod, tiles_per_mod, norm_g, w_main, w_dt, l, tm):
    n = x.shape[0]
    mod_rows = mod.shape[2]

    def mod_spec(chunk):
        return pl.BlockSpec((None, None, mod_rows, D_MODEL),
                            lambda i, j: (l, i // tiles_per_mod, 0, chunk))

    return pl.pallas_call(
        _in_proj_kernel,
        out_shape=[jax.ShapeDtypeStruct((n, PROJ_COLS), F32), jax.ShapeDtypeStruct((n, LANES), F32)],
        grid=(n // tm, PROJ_COLS // _IN_TILE),
        in_specs=[pl.BlockSpec((tm, D_MODEL), lambda i, j: (i, 0)),
                  pl.BlockSpec((None, 1, D_MODEL), lambda i, j: (l, 0, 0)),
                  mod_spec(0), mod_spec(1),
                  pl.BlockSpec((None, D_MODEL, _IN_TILE), lambda i, j: (l, 0, j)),
                  pl.BlockSpec((None, D_MODEL, LANES), lambda i, j: (l, 0, 0))],
        out_specs=[pl.BlockSpec((tm, _IN_TILE), lambda i, j: (i, j)),
                   pl.BlockSpec((tm, LANES), lambda i, j: (i, 0))],
        scratch_shapes=[pltpu.VMEM((tm, D_MODEL), BF16)],
        compiler_params=_params("arbitrary", "arbitrary"),
        name="in_proj",
    )(x, norm_g, mod, mod, w_main, w_dt)


def _up_proj_kernel(x_ref, g_ref, sh_ref, sc_ref, w_ref, u_ref, h_ref):
    @pl.when(pl.program_id(1) == 0)
    def _():
        h_ref[...] = _mod_norm(x_ref[...], g_ref[...], sh_ref[...], sc_ref[...]).astype(BF16)

    u_ref[...] = _dot(h_ref[...], w_ref[...])


def _up_proj(x, mod, tiles_per_mod, norm_g, w_up, l, tm):
    n = x.shape[0]
    mod_rows = mod.shape[2]
    tn = D_FF

    def mod_spec(chunk):
        return pl.BlockSpec((None, None, mod_rows, D_MODEL),
                            lambda i, j: (l, i // tiles_per_mod, 0, chunk))

    return pl.pallas_call(
        _up_proj_kernel,
        out_shape=jax.ShapeDtypeStruct((n, 2 * D_FF), F32),
        grid=(n // tm, 2 * D_FF // tn),
        in_specs=[pl.BlockSpec((tm, D_MODEL), lambda i, j: (i, 0)),
                  pl.BlockSpec((None, 1, D_MODEL), lambda i, j: (l, 0, 0)),
                  mod_spec(3), mod_spec(4),
                  pl.BlockSpec((None, D_MODEL, tn), lambda i, j: (l, 0, j))],
        out_specs=pl.BlockSpec((tm, tn), lambda i, j: (i, j)),
        scratch_shapes=[pltpu.VMEM((tm, D_MODEL), BF16)],
        compiler_params=_params("arbitrary", "arbitrary"),
        name="up_proj",
    )(x, norm_g, mod, mod, w_up)


def _expand_heads(v, g, width_iota_seg):
    rows = v.shape[0]
    out = jnp.broadcast_to(v[:, 4 * g + 3:4 * g + 4], (rows, GROUP_W))
    for r in (2, 1, 0):
        out = jnp.where(width_iota_seg == r, v[:, 4 * g + r:4 * g + r + 1], out)
    return out


def _ssd_prompt_kernel(xbc_ref, dt_ref, z_ref, cw_ref, cb_ref, dtb_ref, alog_ref, dskip_ref, gn_ref,
                       y_ref, ssm_ref, cbuf, xc, st, ybuf):
    c = pl.program_id(1)
    ch = SSD_CHUNK
    pad = 8

    @pl.when(c == 0)
    def _():
        cbuf[0:pad, :] = jnp.zeros((pad, CONV_DIM), F32)
        st[...] = jnp.zeros_like(st)

    cbuf[pad:pad + ch, :] = xbc_ref[...]
    cstep = 512
    for j in range(0, CONV_DIM, cstep):
        sl = slice(j, j + cstep)
        acc = cb_ref[:, sl] + cw_ref[0:1, sl] * cbuf[pad - 3:pad - 3 + ch, sl]
        for i in range(1, SSD_CONV_W):
            acc = acc + cw_ref[i:i + 1, sl] * cbuf[pad - 3 + i:pad - 3 + i + ch, sl]
        xc[:, sl] = _silu(acc)
    cbuf[0:pad, :] = cbuf[ch:ch + pad, :]

    dt = _softplus(dt_ref[...] + dtb_ref[...])
    dta = dt * (-jnp.exp(alog_ref[...]))
    ti = _iota((ch, ch), 0)
    si = _iota((ch, ch), 1)
    causal = si <= ti
    acs = jnp.dot(causal.astype(F32), dta, preferred_element_type=F32,
                  precision=lax.Precision.HIGHEST)
    acs_t = acs.T
    eacs = jnp.exp(acs)
    de = jnp.exp(acs[ch - 1:ch, :] - acs)
    dtde = dt * de
    seg = _iota((ch, GROUP_W), 1) // SSD_HEAD_DIM

    for g in range(N_GROUPS):
        b_g = xc[:, D_INNER + g * D_STATE:D_INNER + (g + 1) * D_STATE].astype(BF16)
        c_g = xc[:, D_INNER + BC_DIM + g * D_STATE:D_INNER + BC_DIM + (g + 1) * D_STATE].astype(BF16)
        cb = _dot_nt(c_g, b_g)
        xs_g = xc[:, g * GROUP_W:(g + 1) * GROUP_W]
        xdt = (xs_g * _expand_heads(dt, g, seg)).astype(BF16)
        xde = (xs_g * _expand_heads(dtde, g, seg)).astype(BF16)
        h_g = st[g * GROUP_W:(g + 1) * GROUP_W, :]
        y_g = _dot_nt(c_g, h_g.astype(BF16)) * _expand_heads(eacs, g, seg)
        for r in range(HEADS_PER_GROUP):
            h = HEADS_PER_GROUP * g + r
            diff = acs[:, h:h + 1] - acs_t[h:h + 1, :]
            m = (cb * jnp.exp(jnp.where(causal, diff, -jnp.inf))).astype(BF16)
            y_g = y_g + jnp.where(seg == r, _dot(m, xdt), 0.0)
        ybuf[:, g * GROUP_W:(g + 1) * GROUP_W] = y_g
        upd = _dot_tn(xde, b_g)
        cd = jnp.concatenate(
            [jnp.broadcast_to(eacs[ch - 1:ch, HEADS_PER_GROUP * g + r:HEADS_PER_GROUP * g + r + 1],
                              (SSD_HEAD_DIM, D_STATE)) for r in range(HEADS_PER_GROUP)], axis=0)
        st[g * GROUP_W:(g + 1) * GROUP_W, :] = h_g * cd + upd

    y = ybuf[...] + dskip_ref[...] * xc[:, 0:D_INNER]
    yg = y * _silu(z_ref[...])
    y_ref[...] = yg * lax.rsqrt(jnp.mean(yg * yg, axis=-1, keepdims=True) + EPS) * gn_ref[...]

    @pl.when(c == pl.num_programs(1) - 1)
    def _():
        ssm_ref[...] = st[...]


def _ssd_prompt(proj, dt, conv_w, conv_b, dt_bias, a_log, d_skip, ssd_norm, l, batch, seq):
    ch = SSD_CHUNK
    nc = seq // ch
    z_blk = COL_Z // D_INNER

    def par(shape):
        return pl.BlockSpec((None,) + shape, lambda b, c: (l, 0, 0))

    return pl.pallas_call(
        _ssd_prompt_kernel,
        out_shape=[jax.ShapeDtypeStruct((batch * seq, D_INNER), F32),
                   jax.ShapeDtypeStruct((batch, N_SSD_HEADS * SSD_HEAD_DIM, D_STATE), F32)],
        grid=(batch, nc),
        in_specs=[pl.BlockSpec((ch, CONV_DIM), lambda b, c: (b * nc + c, COL_XBC // CONV_DIM)),
                  pl.BlockSpec((ch, LANES), lambda b, c: (b * nc + c, 0)),
                  pl.BlockSpec((ch, D_INNER), lambda b, c: (b * nc + c, z_blk)),
                  par((SSD_CONV_W, CONV_DIM)), par((1, CONV_DIM)), par((1, LANES)), par((1, LANES)),
                  par((1, D_INNER)), par((1, D_INNER))],
        out_specs=[pl.BlockSpec((ch, D_INNER), lambda b, c: (b * nc + c, 0)),
                   pl.BlockSpec((None, N_SSD_HEADS * SSD_HEAD_DIM, D_STATE), lambda b, c: (b, 0, 0))],
        scratch_shapes=[pltpu.VMEM((ch + 8, CONV_DIM), F32),
                        pltpu.VMEM((ch, CONV_DIM), F32),
                        pltpu.VMEM((N_SSD_HEADS * SSD_HEAD_DIM, D_STATE), F32),
                        pltpu.VMEM((ch, D_INNER), F32)],
        compiler_params=_params("arbitrary", "arbitrary"),
        name="ssd_prompt",
    )(proj, dt, proj, conv_w, conv_b, dt_bias, a_log, d_skip, ssd_norm)


_QB = 256
_PAIRS = 4
_HW = _PAIRS * LANES


def _sb_prompt_kernel(q_ref, k_ref, v_ref, b_ref, o_ref, kt_ref, vb_ref, acc_ref, carry_ref):
    qi = pl.program_id(2)
    blk = _QB
    n_kb = k_ref.shape[0] // blk

    @pl.when(qi == 0)
    def _():
        for j in range(n_kb):
            kt_ref[j] = k_ref[j * blk:(j + 1) * blk, :].T.astype(BF16)
            vb_ref[j] = v_ref[j * blk:(j + 1) * blk, :].astype(BF16)

    first_head = _iota((blk, LANES), 1) < ATT_HEAD_DIM
    jj = _iota((blk, blk), 0)
    ss = _iota((blk, blk), 1)
    strictly_lower = ss < jj
    later = (jj > ss).astype(BF16)
    qm = []
    for pi in range(_PAIRS):
        q2 = q_ref[:, pi * LANES:(pi + 1) * LANES] * (ATT_HEAD_DIM ** -0.5)
        qm.append(jnp.where(first_head, q2, 0.0).astype(BF16))
        qm.append(jnp.where(first_head, 0.0, q2).astype(BF16))
    acc_ref[...] = jnp.zeros_like(acc_ref)
    carry_ref[...] = jnp.zeros_like(carry_ref)

    def block(kj, masked):
        heads = range(2 * _PAIRS)
        cols = [slice((h // 2) * LANES, (h // 2 + 1) * LANES) for h in heads]
        z = [_dot(qm[h], kt_ref[kj, cols[h], :]) + b_ref[h:h + 1, :] for h in heads]
        sp = [jnp.maximum(z[h], 0.0) + jnp.log(1.0 + jnp.exp(-jnp.abs(z[h]))) for h in heads]
        if masked:
            sp = [jnp.where(strictly_lower, sp[h], 0.0) for h in heads]
        parts = [_split2(sp[h]) for h in heads]
        suffix = [_dot(parts[h][0], later) + _dot(parts[h][1], later) for h in heads]
        carry = [carry_ref[h] for h in heads]
        w = [jnp.exp(z[h] - sp[h] - suffix[h] - jnp.concatenate([carry[h], carry[h]], axis=1)) for h in heads]
        if masked:
            w = [jnp.where(strictly_lower, w[h], 0.0) for h in heads]
        pv = [_dot(w[h].astype(BF16), vb_ref[kj, :, cols[h]]) for h in heads]
        for h in heads:
            acc_ref[h] += pv[h]
            carry_ref[h] = carry[h] + jnp.broadcast_to(suffix[h][:, 0:1] + sp[h][:, 0:1], (blk, LANES))

    block(qi, True)

    def body(t, c):
        block(qi - 1 - t, False)
        return c

    lax.fori_loop(0, qi, body, 0)
    for pi in range(_PAIRS):
        o_ref[:, pi * LANES:(pi + 1) * LANES] = jnp.where(first_head, acc_ref[2 * pi], acc_ref[2 * pi + 1])


def _sb_prompt(proj, bias2, l, batch, seq):
    nq = seq // _QB
    groups = D_ATT // _HW
    return pl.pallas_call(
        _sb_prompt_kernel,
        out_shape=jax.ShapeDtypeStruct((batch * seq, D_ATT), F32),
        grid=(batch, groups, nq),
        in_specs=[pl.BlockSpec((_QB, _HW), lambda b, g, i: (b * nq + i, COL_Q // _HW + g)),
                  pl.BlockSpec((seq, _HW), lambda b, g, i: (b, COL_K // _HW + g)),
                  pl.BlockSpec((seq, _HW), lambda b, g, i: (b, COL_V // _HW + g)),
                  pl.BlockSpec((None, None, 2 * _PAIRS, _QB), lambda b, g, i: (l, g, 0, 0))],
        out_specs=pl.BlockSpec((_QB, _HW), lambda b, g, i: (b * nq + i, g)),
        scratch_shapes=[pltpu.VMEM((nq, _HW, _QB), BF16),
                        pltpu.VMEM((nq, _QB, _HW), BF16),
                        pltpu.VMEM((2 * _PAIRS, _QB, LANES), F32),
                        pltpu.VMEM((2 * _PAIRS, _QB, LANES), F32)],
        compiler_params=_params("arbitrary", "arbitrary", "arbitrary"),
        name="sb_prompt",
    )(proj, proj, proj, bias2)


_TSLOT = 8
_HGRP = 4


def _sb_sample_kernel(pt_ref, qbd_ref, qrow_ref, kn_ref, vn_ref, brep_ref, ka_ref, kb_ref, va_ref, vb_ref,
                      out_ref, acc_ref, carry_ref, onew_ref):
    del pt_ref
    p = pl.program_id(1)
    t_new = kn_ref.shape[0]
    page = ka_ref.shape[1]
    n_grp = N_ATT_HEADS // _HGRP
    grp = _HGRP * _TSLOT
    gw = _HGRP * ATT_HEAD_DIM
    scale = ATT_HEAD_DIM ** -0.5

    @pl.when(p == 0)
    def _():
        acc_ref[...] = jnp.zeros_like(acc_ref)
        q = qrow_ref[...]
        bias = brep_ref[:, 0:1]
        t_of_row = _iota((LANES, 1), 0) % _TSLOT
        z, sp, vis = [], [], []
        for s in range(t_new):
            z.append(jnp.sum(q * kn_ref[s], axis=1, keepdims=True) * scale + bias)
            sp.append(_softplus(z[s]))
            vis.append(s < t_of_row)
        suffix = jnp.zeros((LANES, 1), F32)
        o_new = jnp.zeros((LANES, LANES), F32)
        for s in reversed(range(t_new)):
            w = jnp.where(vis[s], jnp.exp(z[s] - sp[s] - suffix), 0.0)
            o_new = o_new + w * vn_ref[s]
            suffix = suffix + jnp.where(vis[s], sp[s], 0.0)
        onew_ref[...] = o_new
        carry_ref[...] = jnp.broadcast_to(suffix, (LANES, LANES))

    @pl.when(p > 0)
    def _():
        jj = _iota((page, 2 * page), 0)
        ss = _iota((page, 2 * page), 1)
        tri = ((jj > ss) | (ss >= page)).astype(BF16)
        tri2 = jnp.concatenate([tri, tri], axis=0)
        qbd = qbd_ref[...]
        pages = ((ka_ref, va_ref), (kb_ref, vb_ref))
        kt = [k_ref[...].astype(BF16) for k_ref, _ in pages]
        z = [jnp.concatenate(
            [_dot(qbd[grp * g:grp * (g + 1), gw * g:gw * (g + 1)], kt[i][gw * g:gw * (g + 1), :])
             for g in range(n_grp)], axis=0) * scale + brep_ref[...] for i in range(2)]
        sp = [_softplus(z[i]) for i in range(2)]
        s2 = [_dot(jnp.concatenate(_split2(sp[i]), axis=1), tri2) for i in range(2)]
        carry = [carry_ref[...]]
        carry.append(carry[0] + s2[0][:, page:2 * page])
        wb = [jnp.exp(z[i] - sp[i] - s2[i][:, 0:page] - carry[i]).astype(BF16) for i in range(2)]
        vt = [v_ref[...].astype(BF16) for _, v_ref in pages]
        for g in range(n_grp):
            acc_ref[g] += (_dot_nt(wb[0][grp * g:grp * (g + 1), :], vt[0][gw * g:gw * (g + 1), :])
                           + _dot_nt(wb[1][grp * g:grp * (g + 1), :], vt[1][gw * g:gw * (g + 1), :]))
        carry_ref[...] = carry[1] + s2[1][:, page:2 * page]

    @pl.when(p == pl.num_programs(1) - 1)
    def _():
        rr = _iota((LANES, gw), 0)
        cc = _iota((LANES, gw), 1)
        rep = ((rr < ATT_HEAD_DIM) & (cc % ATT_HEAD_DIM == rr)).astype(BF16)
        for g in range(n_grp):
            parts = _split3(onew_ref[grp * g:grp * (g + 1), :])
            out_ref[g] = acc_ref[g] + _dot(parts[0], rep) + _dot(parts[1], rep) + _dot(parts[2], rep)


def _sb_sample(page_table, qbd, qrow, kn_exp, vn_exp, brep, cache_kt, cache_vt, l):
    nb = qbd.shape[0]
    t_new = kn_exp.shape[1]
    n_pages = page_table.shape[1]
    page = cache_kt.shape[3]
    n_grp = N_ATT_HEADS // _HGRP

    def page_map(second):
        def index_map(b, p, pt):
            first = n_pages + 1 - 2 * jnp.maximum(p, 1)
            return (l, pt[b, first - second], 0, 0)
        return pl.BlockSpec((None, None, D_ATT, page), index_map)

    def per_seq(shape):
        return pl.BlockSpec((None,) + shape, lambda b, p, pt: (b,) + (0,) * len(shape))

    grid_spec = pltpu.PrefetchScalarGridSpec(
        num_scalar_prefetch=1,
        grid=(nb, n_pages // 2 + 1),
        in_specs=[per_seq((LANES, D_ATT)), per_seq((LANES, ATT_HEAD_DIM)),
                  per_seq((t_new, LANES, ATT_HEAD_DIM)), per_seq((t_new, LANES, LANES)),
                  pl.BlockSpec((None, LANES, LANES), lambda b, p, pt: (l, 0, 0)),
                  page_map(0), page_map(1), page_map(0), page_map(1)],
        out_specs=per_seq((n_grp, _HGRP * _TSLOT, _HGRP * ATT_HEAD_DIM)),
        scratch_shapes=[pltpu.VMEM((n_grp, _HGRP * _TSLOT, _HGRP * ATT_HEAD_DIM), F32),
                        pltpu.VMEM((LANES, LANES), F32),
                        pltpu.VMEM((LANES, LANES), F32)])
    return pl.pallas_call(
        _sb_sample_kernel,
        out_shape=jax.ShapeDtypeStruct((nb, n_grp, _HGRP * _TSLOT, _HGRP * ATT_HEAD_DIM), F32),
        grid_spec=grid_spec,
        compiler_params=_params("arbitrary", "arbitrary"),
        name="sb_sample",
    )(page_table, qbd, qrow, kn_exp, vn_exp, brep, cache_kt, cache_kt, cache_vt, cache_vt)


def _ssd_sample_pre_kernel(xbc_ref, prev_ref, dt_ref, cw_ref, cb_ref, dtb_ref, alog_ref, seg_ref, exp_ref,
                           xc_ref, yd_ref, xd_ref, ea_ref, cd_ref):
    t_new = xbc_ref.shape[0]
    taps = SSD_CONV_W

    def tap(r, sl):
        return prev_ref[r, :, sl] if r < taps - 1 else xbc_ref[r - (taps - 1), :, sl]

    cstep = 512
    for t in range(t_new):
        for j in range(0, CONV_DIM, cstep):
            sl = slice(j, j + cstep)
            acc = cb_ref[:, sl] + cw_ref[0:1, sl] * tap(t, sl)
            for i in range(1, taps):
                acc = acc + cw_ref[i:i + 1, sl] * tap(t + i, sl)
            xc_ref[t, :, sl] = _silu(acc)

    a = -jnp.exp(alog_ref[...])
    dt = [_softplus(dt_ref[t] + dtb_ref[...]) for t in range(t_new)]
    acs = []
    for t in range(t_new):
        step = dt[t] * a
        acs.append(step if t == 0 else acs[t - 1] + step)

    def expand(v):
        e = exp_ref[...]
        hi, mid, lo = _split3(v)
        return _dot(hi, e) + _dot(mid, e) + _dot(lo, e)

    seg = seg_ref[...]
    for t in range(t_new):
        c_t = xc_ref[t, :, D_INNER + BC_DIM:CONV_DIM]
        yd = None
        for s in range(t + 1):
            b_s = xc_ref[s, :, D_INNER:D_INNER + BC_DIM]
            hi, lo = _split2(c_t * b_s)
            cbh = _dot(hi, seg) + _dot(lo, seg)
            coef = cbh * jnp.exp(acs[t] - acs[s]) * dt[s]
            term = expand(coef) * xc_ref[s, :, 0:D_INNER]
            yd = term if yd is None else yd + term
        yd_ref[t] = yd
        ea_ref[t] = expand(jnp.exp(acs[t]))
        xd_ref[t] = expand(dt[t] * jnp.exp(acs[t_new - 1] - acs[t])) * xc_ref[t, :, 0:D_INNER]
    cd_ref[...] = jnp.exp(acs[t_new - 1])


def _ssd_sample_pre(xbc_tm, prev_tm, dt_tm, conv_w, conv_b, dt_bias, a_log, seg_mat, exp_mat, l):
    t_new, nb, _ = xbc_tm.shape
    bt = 32

    def par(shape):
        return pl.BlockSpec((None,) + shape, lambda i: (l, 0, 0))

    def tm_spec(t, w):
        return pl.BlockSpec((t, bt, w), lambda i: (0, i, 0))

    return pl.pallas_call(
        _ssd_sample_pre_kernel,
        out_shape=[jax.ShapeDtypeStruct((t_new, nb, CONV_DIM), F32),
                   jax.ShapeDtypeStruct((t_new, nb, D_INNER), F32),
                   jax.ShapeDtypeStruct((t_new, nb, D_INNER), F32),
                   jax.ShapeDtypeStruct((t_new, nb, D_INNER), F32),
                   jax.ShapeDtypeStruct((nb, LANES), F32)],
        grid=(nb // bt,),
        in_specs=[tm_spec(t_new, CONV_DIM), tm_spec(SSD_CONV_W - 1, CONV_DIM), tm_spec(t_new, LANES),
                  par((SSD_CONV_W, CONV_DIM)), par((1, CONV_DIM)), par((1, LANES)), par((1, LANES)),
                  pl.BlockSpec((BC_DIM, LANES), lambda i: (0, 0)),
                  pl.BlockSpec((LANES, D_INNER), lambda i: (0, 0))],
        out_specs=[tm_spec(t_new, CONV_DIM), tm_spec(t_new, D_INNER), tm_spec(t_new, D_INNER),
                   tm_spec(t_new, D_INNER), pl.BlockSpec((bt, LANES), lambda i: (i, 0))],
        compiler_params=_params("arbitrary"),
        name="ssd_sample_pre",
    )(xbc_tm, prev_tm, dt_tm, conv_w, conv_b, dt_bias, a_log, seg_mat, exp_mat)


_TPAD = 16


def _ssd_sample_state_kernel(st_ref, bc_ref, xd_ref, cd_ref, new_ref, yoff_ref):
    bb = st_ref.shape[0]
    for jb in range(bb):
        bc = bc_ref[jb]
        xd = xd_ref[jb].astype(BF16)
        cdb = cd_ref[jb]
        for g in range(N_GROUPS):
            rows = slice(g * GROUP_W, (g + 1) * GROUP_W)
            h_g = st_ref[jb, rows, :]
            b_g = bc[:, g * D_STATE:(g + 1) * D_STATE].astype(BF16)
            c_g = bc[:, BC_DIM + g * D_STATE:BC_DIM + (g + 1) * D_STATE].astype(BF16)
            yoff_ref[jb, :, rows] = _dot_nt(c_g, h_g.astype(BF16))
            upd = _dot_tn(xd[:, rows], b_g)
            cd = jnp.concatenate(
                [jnp.broadcast_to(cdb[0:1, HEADS_PER_GROUP * g + r:HEADS_PER_GROUP * g + r + 1],
                                  (SSD_HEAD_DIM, D_STATE)) for r in range(HEADS_PER_GROUP)], axis=0)
            new_ref[jb, rows, :] = h_g * cd + upd


def _ssd_sample_state(state_ssm, bc_bm, xd_bm, cd, l):
    nb = bc_bm.shape[0]
    bb = 4
    rows = N_SSD_HEADS * SSD_HEAD_DIM
    return pl.pallas_call(
        _ssd_sample_state_kernel,
        out_shape=[jax.ShapeDtypeStruct((nb, rows, D_STATE), F32),
                   jax.ShapeDtypeStruct((nb, _TPAD, D_INNER), F32)],
        grid=(nb // bb,),
        in_specs=[pl.BlockSpec((None, bb, rows, D_STATE), lambda i: (l, i, 0, 0)),
                  pl.BlockSpec((bb, _TPAD, 2 * BC_DIM), lambda i: (i, 0, 0)),
                  pl.BlockSpec((bb, _TPAD, D_INNER), lambda i: (i, 0, 0)),
                  pl.BlockSpec((bb, 1, LANES), lambda i: (i, 0, 0))],
        out_specs=[pl.BlockSpec((bb, rows, D_STATE), lambda i: (i, 0, 0)),
                   pl.BlockSpec((bb, _TPAD, D_INNER), lambda i: (i, 0, 0))],
        compiler_params=_params("arbitrary"),
        name="ssd_sample_state",
    )(state_ssm, bc_bm, xd_bm, cd)


def _ssd_sample_post_kernel(yd_ref, ea_ref, yoff_ref, xs_ref, z_ref, dskip_ref, gn_ref, y_ref):
    y = yd_ref[...] + ea_ref[...] * yoff_ref[...] + dskip_ref[...] * xs_ref[...]
    yg = y * _silu(z_ref[...])
    y_ref[...] = yg * lax.rsqrt(jnp.mean(yg * yg, axis=-1, keepdims=True) + EPS) * gn_ref[...]


def _ssd_sample_post(yd, ea, yoff, xc, proj, d_skip, ssd_norm, l):
    n = yd.shape[0]
    tm = 128
    row = pl.BlockSpec((tm, D_INNER), lambda i: (i, 0))
    z_row = pl.BlockSpec((tm, D_INNER), lambda i: (i, COL_Z // D_INNER))
    par = pl.BlockSpec((None, 1, D_INNER), lambda i: (l, 0, 0))
    return pl.pallas_call(
        _ssd_sample_post_kernel,
        out_shape=jax.ShapeDtypeStruct((n, D_INNER), F32),
        grid=(n // tm,),
        in_specs=[row, row, row, row, z_row, par, par],
        out_specs=row,
        compiler_params=_params("arbitrary"),
        name="ssd_sample_post",
    )(yd, ea, yoff, xc, proj, d_skip, ssd_norm)


def _mix_kernel(y_ref, o_ref, gs_ref, ga_ref, x_ref, g1_ref, wbs_ref, wba_ref, wo_ref, out_ref):
    a = _dot(y_ref[...].astype(BF16), wbs_ref[...])
    b = _dot(o_ref[...].astype(BF16), wba_ref[...])
    merged = jax.nn.sigmoid(gs_ref[...]) * a + jax.nn.sigmoid(ga_ref[...]) * b
    mix = _dot(merged.astype(BF16), wo_ref[...])
    out_ref[...] = x_ref[...] + g1_ref[...] * mix


def _mix(y, o, proj, x, mod, tiles_per_mod, w_br_ssd, w_br_att, w_out, l, tm):
    n = x.shape[0]
    mod_rows = mod.shape[2]

    def row(w, col=0):
        return pl.BlockSpec((tm, w), lambda i: (i, col // w))

    def wspec(k):
        return pl.BlockSpec((None, k, D_MODEL), lambda i: (l, 0, 0))

    return pl.pallas_call(
        _mix_kernel,
        out_shape=jax.ShapeDtypeStruct((n, D_MODEL), F32),
        grid=(n // tm,),
        in_specs=[row(D_INNER), row(D_ATT), row(D_MODEL, COL_GS), row(D_MODEL, COL_GA), row(D_MODEL),
                  pl.BlockSpec((None, None, mod_rows, D_MODEL), lambda i: (l, i // tiles_per_mod, 0, 2)),
                  wspec(D_INNER), wspec(D_ATT), wspec(D_MODEL)],
        out_specs=row(D_MODEL),
        compiler_params=_params("arbitrary"),
        name="mix",
    )(y, o, proj, proj, x, mod, w_br_ssd, w_br_att, w_out)


_FFN_STEP = 256


def _ffn_gate(tap_rows, cw_ref, cb_ref, act_ref):
    for j in range(0, D_FF, _FFN_STEP):
        halves = []
        for off in (0, D_FF):
            sl = slice(off + j, off + j + _FFN_STEP)
            acc = cb_ref[:, sl] + cw_ref[0:1, sl] * tap_rows(0, sl)
            for i in range(1, FFN_CONV_W):
                acc = acc + cw_ref[i:i + 1, sl] * tap_rows(i, sl)
            halves.append(acc)
        val, gate = halves
        act_ref[:, j:j + _FFN_STEP] = (_silu(gate) * val).astype(BF16)


def _ffn_prompt_kernel(u_ref, x_ref, g2_ref, cw_ref, cb_ref, wd_ref, out_ref, cbuf, act):
    tm = u_ref.shape[0]
    pad = 8

    @pl.when(pl.program_id(1) == 0)
    def _():
        cbuf[0:pad, :] = jnp.zeros((pad, 2 * D_FF), F32)

    cbuf[pad:pad + tm, :] = u_ref[...]
    first = pad - (FFN_CONV_W - 1)
    _ffn_gate(lambda i, sl: cbuf[first + i:first + i + tm, sl], cw_ref, cb_ref, act)
    cbuf[0:pad, :] = cbuf[tm:tm + pad, :]
    out_ref[...] = x_ref[...] + g2_ref[...] * _dot(act[...], wd_ref[...])


def _ffn_prompt(u, x, mod, ffn_conv_w, ffn_conv_b, w_down, l, batch, seq, tm):
    nt = seq // tm

    def par(shape):
        return pl.BlockSpec((None,) + shape, lambda b, c: (l, 0, 0))

    return pl.pallas_call(
        _ffn_prompt_kernel,
        out_shape=jax.ShapeDtypeStruct((batch * seq, D_MODEL), F32),
        grid=(batch, nt),
        in_specs=[pl.BlockSpec((tm, 2 * D_FF), lambda b, c: (b * nt + c, 0)),
                  pl.BlockSpec((tm, D_MODEL), lambda b, c: (b * nt + c, 0)),
                  pl.BlockSpec((None, None, 1, D_MODEL), lambda b, c: (l, b, 0, 5)),
                  par((FFN_CONV_W, 2 * D_FF)), par((1, 2 * D_FF)), par((D_FF, D_MODEL))],
        out_specs=pl.BlockSpec((tm, D_MODEL), lambda b, c: (b * nt + c, 0)),
        scratch_shapes=[pltpu.VMEM((tm + 8, 2 * D_FF), F32), pltpu.VMEM((tm, D_FF), BF16)],
        compiler_params=_params("arbitrary", "arbitrary"),
        name="ffn_prompt",
    )(u, x, mod, ffn_conv_w, ffn_conv_b, w_down)


def _ffn_sample_kernel(u_ref, prev_ref, x_ref, g2_ref, cw_ref, cb_ref, wd_ref, out_ref, act):
    t_new, nb, _ = u_ref.shape
    n_prev = FFN_CONV_W - 1
    for t in range(t_new):
        def tap_rows(i, sl, t=t):
            src = t + i - n_prev
            return u_ref[src, :, sl] if src >= 0 else prev_ref[src + n_prev, :, sl]

        _ffn_gate(tap_rows, cw_ref, cb_ref, act.at[t * nb:(t + 1) * nb])
    out_ref[...] = x_ref[...] + g2_ref[...] * _dot(act[...], wd_ref[...])


def _ffn_sample(u_tm, prev_tm, x, mod, ffn_conv_w, ffn_conv_b, w_down, l):
    t_new, nb, _ = u_tm.shape
    n = t_new * nb

    def par(shape):
        return pl.BlockSpec((None,) + shape, lambda i: (l, 0, 0))

    return pl.pallas_call(
        _ffn_sample_kernel,
        out_shape=jax.ShapeDtypeStruct((n, D_MODEL), F32),
        grid=(1,),
        in_specs=[pl.BlockSpec((t_new, nb, 2 * D_FF), lambda i: (0, 0, 0)),
                  pl.BlockSpec((FFN_CONV_W - 1, nb, 2 * D_FF), lambda i: (0, 0, 0)),
                  pl.BlockSpec((n, D_MODEL), lambda i: (0, 0)),
                  pl.BlockSpec((None, None, n, D_MODEL), lambda i: (l, 0, 0, 5)),
                  par((FFN_CONV_W, 2 * D_FF)), par((1, 2 * D_FF)), par((D_FF, D_MODEL))],
        out_specs=pl.BlockSpec((n, D_MODEL), lambda i: (0, 0)),
        scratch_shapes=[pltpu.VMEM((n, D_FF), BF16)],
        compiler_params=_params("arbitrary"),
        name="ffn_sample",
    )(u_tm, prev_tm, x, mod, ffn_conv_w, ffn_conv_b, w_down)


def _final_norm_kernel(x_ref, g_ref, o_ref):
    x = x_ref[...]
    o_ref[...] = x * lax.rsqrt(jnp.mean(x * x, axis=-1, keepdims=True) + EPS) * g_ref[...]


def _final_norm(x, g, tm):
    n = x.shape[0]
    return pl.pallas_call(
        _final_norm_kernel,
        out_shape=jax.ShapeDtypeStruct((n, D_MODEL), F32),
        grid=(n // tm,),
        in_specs=[pl.BlockSpec((tm, D_MODEL), lambda i: (i, 0)),
                  pl.BlockSpec((1, D_MODEL), lambda i: (0, 0))],
        out_specs=pl.BlockSpec((tm, D_MODEL), lambda i: (i, 0)),
        compiler_params=_params("arbitrary"),
        name="final_norm",
    )(x, g.reshape(1, D_MODEL))


def _to_batch_major(a_tm, t_new, nb):
    return a_tm.reshape(t_new, nb, -1).transpose(1, 0, 2)


def _pad_time(a_bm):
    return jnp.pad(a_bm, ((0, 0), (0, _TPAD - a_bm.shape[1]), (0, 0)))


def kernel(x_prompt, x_sample, cache_k, cache_v, state_ssm, state_conv, state_ffn_conv, page_table,
           c_prompt, c_sample, w_ada, b_ada, norm_mix, norm_ffn, w_in, conv_w, conv_b, dt_bias, a_log,
           d_skip, ssd_norm, sb_bias, w_br_ssd, w_br_att, w_out, w_up, ffn_conv_w, ffn_conv_b, w_down,
           norm_final):
    batch, seq, _ = x_prompt.shape
    nb, t_new, _ = x_sample.shape
    depth = w_in.shape[0]
    n_s = nb * t_new
    tm_p = 512

    mods = _ada(jnp.concatenate([c_prompt, c_sample], axis=0), w_ada, b_ada)
    mod_p = mods[:, :batch].reshape(depth, batch, 1, N_MOD * D_MODEL)
    mod_s = jnp.tile(mods[:, batch:], (1, t_new, 1)).reshape(depth, 1, n_s, N_MOD * D_MODEL)

    cuts = (D_INNER, D_INNER + CONV_DIM)
    w_main = jnp.concatenate([w_in[:, :, cuts[0]:cuts[1]], w_in[:, :, :cuts[0]],
                              w_in[:, :, cuts[1] + N_SSD_HEADS:]], axis=2).astype(BF16)
    w_dt = jnp.pad(w_in[:, :, cuts[1]:cuts[1] + N_SSD_HEADS],
                   ((0, 0), (0, 0), (0, LANES - N_SSD_HEADS))).astype(BF16)
    w_br_ssd_b = w_br_ssd.astype(BF16)
    w_br_att_b = w_br_att.astype(BF16)
    w_out_b = w_out.astype(BF16)
    w_up_b = w_up.astype(BF16)
    w_down_b = w_down.astype(BF16)

    def row3(a):
        return a.reshape(depth, 1, a.shape[-1])

    def head_row(a):
        return jnp.pad(a, ((0, 0), (0, LANES - N_SSD_HEADS))).reshape(depth, 1, LANES)

    norm_mix3, norm_ffn3 = row3(norm_mix), row3(norm_ffn)
    conv_b3, ffn_conv_b3, ssd_norm3 = row3(conv_b), row3(ffn_conv_b), row3(ssd_norm)
    dt_bias3, a_log3 = head_row(dt_bias), head_row(a_log)
    d_skip3 = row3(jnp.repeat(d_skip, SSD_HEAD_DIM, axis=1))
    bias2 = jnp.broadcast_to(sb_bias[:, :, None], (depth, N_ATT_HEADS, _QB)).reshape(
        depth, N_ATT_HEADS // (2 * _PAIRS), 2 * _PAIRS, _QB)
    brep = jnp.broadcast_to(jnp.repeat(sb_bias, _TSLOT, axis=1)[:, :, None], (depth, LANES, LANES))
    head_eye = jnp.eye(N_ATT_HEADS, dtype=bool)
    seg_mat = (jnp.arange(BC_DIM)[:, None] // D_STATE
               == jnp.where(jnp.arange(LANES) < N_SSD_HEADS, jnp.arange(LANES) // HEADS_PER_GROUP, -1)[None, :]
               ).astype(BF16)
    exp_mat = (jnp.arange(LANES)[:, None] == jnp.arange(D_INNER)[None, :] // SSD_HEAD_DIM).astype(BF16)

    cache_kt = cache_k.transpose(0, 1, 3, 4, 2).reshape(cache_k.shape[:2] + (D_ATT, cache_k.shape[2]))
    cache_vt = cache_v.transpose(0, 1, 3, 4, 2).reshape(cache_v.shape[:2] + (D_ATT, cache_v.shape[2]))
    state_ssm4 = state_ssm.reshape(depth, nb, N_SSD_HEADS * SSD_HEAD_DIM, D_STATE)

    xp = x_prompt.reshape(batch * seq, D_MODEL)
    xs = x_sample.transpose(1, 0, 2).reshape(n_s, D_MODEL)
    outs = {name: [] for name in ("kp", "vp", "sp", "cp", "fp", "ks", "vs", "ss", "cs", "fs")}

    for l in range(depth):
        proj, dt = _in_proj(xp, mod_p, seq // tm_p, norm_mix3, w_main, w_dt, l, tm_p)
        y, ssm_new = _ssd_prompt(proj, dt, conv_w, conv_b3, dt_bias3, a_log3, d_skip3, ssd_norm3,
                                 l, batch, seq)
        o = _sb_prompt(proj, bias2, l, batch, seq)
        xp = _mix(y, o, proj, xp, mod_p, seq // 256, w_br_ssd_b, w_br_att_b, w_out_b, l, 256)
        u = _up_proj(xp, mod_p, seq // tm_p, norm_ffn3, w_up_b, l, tm_p)
        xp = _ffn_prompt(u, xp, mod_p, ffn_conv_w, ffn_conv_b3, w_down_b, l, batch, seq, 256)
        proj3 = proj.reshape(batch, seq, PROJ_COLS)
        outs["kp"].append(proj3[:, :, COL_K:COL_K + D_ATT].reshape(batch, seq, N_ATT_HEADS, ATT_HEAD_DIM))
        outs["vp"].append(proj3[:, :, COL_V:COL_V + D_ATT].reshape(batch, seq, N_ATT_HEADS, ATT_HEAD_DIM))
        outs["sp"].append(ssm_new.reshape(batch, N_SSD_HEADS, SSD_HEAD_DIM, D_STATE))
        outs["cp"].append(proj3[:, seq - (SSD_CONV_W - 1):, COL_XBC:COL_XBC + CONV_DIM])
        outs["fp"].append(u.reshape(batch, seq, 2 * D_FF)[:, seq - (FFN_CONV_W - 1):])

        proj, dt = _in_proj(xs, mod_s, 1, norm_mix3, w_main, w_dt, l, n_s)
        q, k, v = (proj[:, c:c + D_ATT] for c in (COL_Q, COL_K, COL_V))
        xbc_tm = proj[:, COL_XBC:COL_XBC + CONV_DIM].reshape(t_new, nb, CONV_DIM)
        xc, yd, xd, ea, cd = _ssd_sample_pre(
            xbc_tm, state_conv[l].transpose(1, 0, 2), dt.reshape(t_new, nb, LANES),
            conv_w, conv_b3, dt_bias3, a_log3, seg_mat, exp_mat, l)
        bc_bm = _pad_time(xc[:, :, D_INNER:].transpose(1, 0, 2))
        xd_bm = _pad_time(xd.transpose(1, 0, 2))
        ssm_new, yoff_bm = _ssd_sample_state(state_ssm4, bc_bm, xd_bm, cd.reshape(nb, 1, LANES), l)
        yoff = yoff_bm[:, :t_new].transpose(1, 0, 2).reshape(n_s, D_INNER)
        y = _ssd_sample_post(yd.reshape(n_s, D_INNER), ea.reshape(n_s, D_INNER), yoff,
                             xc.reshape(n_s, CONV_DIM), proj, d_skip3, ssd_norm3, l)
        k_bm, v_bm = (_to_batch_major(a, t_new, nb) for a in (k, v))
        q5 = jnp.pad(q.reshape(t_new, nb, N_ATT_HEADS, ATT_HEAD_DIM).transpose(1, 2, 0, 3),
                     ((0, 0), (0, 0), (0, _TSLOT - t_new), (0, 0)))
        qrow = q5.reshape(nb, LANES, ATT_HEAD_DIM)
        qbd = jnp.where(head_eye[None, :, None, :, None], q5[:, :, :, None, :], 0.0).reshape(
            nb, LANES, D_ATT).astype(BF16)
        kn_exp, vn_exp = (
            jnp.broadcast_to(a.reshape(nb, t_new, N_ATT_HEADS, 1, ATT_HEAD_DIM),
                             (nb, t_new, N_ATT_HEADS, _TSLOT, ATT_HEAD_DIM)).reshape(
                                 nb, t_new, LANES, ATT_HEAD_DIM) for a in (k_bm, v_bm))
        vn_exp = jnp.pad(vn_exp, ((0, 0), (0, 0), (0, 0), (0, LANES - ATT_HEAD_DIM)))
        o_raw = _sb_sample(page_table, qbd, qrow, kn_exp, vn_exp, brep, cache_kt, cache_vt, l)
        o_raw = o_raw.reshape(nb, N_ATT_HEADS // _HGRP, _HGRP, _TSLOT, _HGRP, ATT_HEAD_DIM)
        o_bm = jnp.moveaxis(jnp.diagonal(o_raw, axis1=2, axis2=4), -1, 2)[:, :, :, :t_new]
        o = o_bm.reshape(nb, N_ATT_HEADS, t_new, ATT_HEAD_DIM).transpose(2, 0, 1, 3).reshape(n_s, D_ATT)
        xs = _mix(y, o, proj, xs, mod_s, 1, w_br_ssd_b, w_br_att_b, w_out_b, l, n_s)
        u = _up_proj(xs, mod_s, 1, norm_ffn3, w_up_b, l, n_s)
        u_tm = u.reshape(t_new, nb, 2 * D_FF)
        xs = _ffn_sample(u_tm, state_ffn_conv[l].transpose(1, 0, 2), xs, mod_s,
                         ffn_conv_w, ffn_conv_b3, w_down_b, l)
        outs["ks"].append(k_bm.reshape(nb, t_new, N_ATT_HEADS, ATT_HEAD_DIM))
        outs["vs"].append(v_bm.reshape(nb, t_new, N_ATT_HEADS, ATT_HEAD_DIM))
        outs["ss"].append(ssm_new.reshape(nb, N_SSD_HEADS, SSD_HEAD_DIM, D_STATE))
        outs["cs"].append(xbc_tm[t_new - (SSD_CONV_W - 1):].transpose(1, 0, 2))
        outs["fs"].append(u_tm[t_new - (FFN_CONV_W - 1):].transpose(1, 0, 2))

    y_prompt = _final_norm(xp, norm_final, 512).reshape(batch, seq, D_MODEL)
    y_sample = _final_norm(xs, norm_final, n_s).reshape(t_new, nb, D_MODEL).transpose(1, 0, 2)
    st = {name: jnp.stack(vals) for name, vals in outs.items()}
    return (y_prompt, y_sample, st["kp"], st["vp"], st["sp"], st["cp"], st["fp"],
            st["ks"], st["vs"], st["ss"], st["cs"], st["fs"])
```

```python
import functools

import jax
import jax.numpy as jnp
from jax import lax
from jax.experimental import pallas as pl
from jax.experimental.pallas import tpu as pltpu

F32 = jnp.float32
BF16 = jnp.bfloat16

D_MODEL = 1024
D_INNER = 2048
SSD_HEAD_DIM = 64
N_SSD_HEADS = 32
D_STATE = 128
N_GROUPS = 8
HEADS_PER_GROUP = N_SSD_HEADS // N_GROUPS
GROUP_W = HEADS_PER_GROUP * SSD_HEAD_DIM
SSD_CONV_W = 4
BC_DIM = N_GROUPS * D_STATE
CONV_DIM = D_INNER + 2 * BC_DIM
SSD_CHUNK = 128
ATT_HEAD_DIM = 64
N_ATT_HEADS = 16
D_ATT = 1024
D_FF = 2816
FFN_CONV_W = 3
N_MOD = 6
EPS = 1e-6
LANES = 128
VMEM_LIMIT = 56 * 1024 * 1024


def _params(*sem):
    return pltpu.CompilerParams(dimension_semantics=sem, vmem_limit_bytes=VMEM_LIMIT)


def _dot(a, b):
    return jnp.dot(a, b, preferred_element_type=F32)


def _dot_nt(a, b):
    return lax.dot_general(a, b, (((1,), (1,)), ((), ())), preferred_element_type=F32)


def _dot_tn(a, b):
    return lax.dot_general(a, b, (((0,), (0,)), ((), ())), preferred_element_type=F32)


def _silu(x):
    return x * jax.nn.sigmoid(x)


def _softplus(x):
    return jnp.maximum(x, 0.0) + jnp.log1p(jnp.exp(-jnp.abs(x)))


def _split2(x):
    hi = x.astype(BF16)
    lo = (x - hi.astype(F32)).astype(BF16)
    return hi, lo


def _split3(x):
    hi = x.astype(BF16)
    r = x - hi.astype(F32)
    mid = r.astype(BF16)
    lo = (r - mid.astype(F32)).astype(BF16)
    return hi, mid, lo


def _iota(shape, dim):
    return lax.broadcasted_iota(jnp.int32, shape, dim)


def _mod_norm(x, gain, shift, scale):
    y = x * lax.rsqrt(jnp.mean(x * x, axis=-1, keepdims=True) + EPS)
    return (y * gain) * (1.0 + scale) + shift


def _ada_kernel(c_ref, w_ref, b_ref, o_ref):
    a = _silu(c_ref[...]).astype(BF16)
    o_ref[...] = _dot(a, w_ref[...].astype(BF16)) + b_ref[...]


def _ada(c_all, w_ada, b_ada):
    depth, _, cols = w_ada.shape
    rows = c_all.shape[0]
    tn = 1536
    return pl.pallas_call(
        _ada_kernel,
        out_shape=jax.ShapeDtypeStruct((depth, rows, cols), F32),
        grid=(depth, cols // tn),
        in_specs=[pl.BlockSpec((rows, D_MODEL), lambda l, j: (0, 0)),
                  pl.BlockSpec((None, D_MODEL, tn), lambda l, j: (l, 0, j)),
                  pl.BlockSpec((None, 1, tn), lambda l, j: (l, 0, j))],
        out_specs=pl.BlockSpec((None, rows, tn), lambda l, j: (l, 0, j)),
        compiler_params=_params("arbitrary", "arbitrary"),
        name="ada",
    )(c_all, w_ada, b_ada.reshape(depth, 1, cols))


_IN_TILE = 1024
PROJ_COLS = CONV_DIM + D_INNER + 3 * D_ATT + 2 * D_MODEL
COL_XBC = 0
COL_Z = COL_XBC + CONV_DIM
COL_Q = COL_Z + D_INNER
COL_K = COL_Q + D_ATT
COL_V = COL_K + D_ATT
COL_GS = COL_V + D_ATT
COL_GA = COL_GS + D_MODEL


def _in_proj_kernel(x_ref, g_ref, sh_ref, sc_ref, w_ref, wdt_ref, proj_ref, dt_ref, h_ref):
    @pl.when(pl.program_id(1) == 0)
    def _():
        h = _mod_norm(x_ref[...], g_ref[...], sh_ref[...], sc_ref[...]).astype(BF16)
        h_ref[...] = h
        dt_ref[...] = _dot(h, wdt_ref[...])

    proj_ref[...] = _dot(h_ref[...], w_ref[...])


def _in_proj(x, mod, tiles_per_mod, norm_g, w_main, w_dt, l, tm):
    n = x.shape[0]
    mod_rows = mod.shape[2]

    def mod_spec(chunk):
        return pl.BlockSpec((None, None, mod_rows, D_MODEL),
                            lambda i, j: (l, i // tiles_per_mod, 0, chunk))

    return pl.pallas_call(
        _in_proj_kernel,
        out_shape=[jax.ShapeDtypeStruct((n, PROJ_COLS), F32), jax.ShapeDtypeStruct((n, LANES), F32)],
        grid=(n // tm, PROJ_COLS // _IN_TILE),
        in_specs=[pl.BlockSpec((tm, D_MODEL), lambda i, j: (i, 0)),
                  pl.BlockSpec((None, 1, D_MODEL), lambda i, j: (l, 0, 0)),
                  mod_spec(0), mod_spec(1),
                  pl.BlockSpec((None, D_MODEL, _IN_TILE), lambda i, j: (l, 0, j)),
                  pl.BlockSpec((None, D_MODEL, LANES), lambda i, j: (l, 0, 0))],
        out_specs=[pl.BlockSpec((tm, _IN_TILE), lambda i, j: (i, j)),
                   pl.BlockSpec((tm, LANES), lambda i, j: (i, 0))],
        scratch_shapes=[pltpu.VMEM((tm, D_MODEL), BF16)],
        compiler_params=_params("arbitrary", "arbitrary"),
        name="in_proj",
    )(x, norm_g, mod, mod, w_main, w_dt)


def _up_proj_kernel(x_ref, g_ref, sh_ref, sc_ref, w_ref, u_ref, h_ref):
    @pl.when(pl.program_id(1) == 0)
    def _():
        h_ref[...] = _mod_norm(x_ref[...], g_ref[...], sh_ref[...], sc_ref[...]).astype(BF16)

    u_ref[...] = _dot(h_ref[...], w_ref[...])


def _up_proj(x, mod, tiles_per_mod, norm_g, w_up, l, tm):
    n = x.shape[0]
    mod_rows = mod.shape[2]
    tn = D_FF

    def mod_spec(chunk):
        return pl.BlockSpec((None, None, mod_rows, D_MODEL),
                            lambda i, j: (l, i // tiles_per_mod, 0, chunk))

    return pl.pallas_call(
        _up_proj_kernel,
        out_shape=jax.ShapeDtypeStruct((n, 2 * D_FF), F32),
        grid=(n // tm, 2 * D_FF // tn),
        in_specs=[pl.BlockSpec((tm, D_MODEL), lambda i, j: (i, 0)),
                  pl.BlockSpec((None, 1, D_MODEL), lambda i, j: (l, 0, 0)),
                  mod_spec(3), mod_spec(4),
                  pl.BlockSpec((None, D_MODEL, tn), lambda i, j: (l, 0, j))],
        out_specs=pl.BlockSpec((tm, tn), lambda i, j: (i, j)),
        scratch_shapes=[pltpu.VMEM((tm, D_MODEL), BF16)],
        compiler_params=_params("arbitrary", "arbitrary"),
        name="up_proj",
    )(x, norm_g, mod, mod, w_up)


def _expand_heads(v, g, width_iota_seg):
    rows = v.shape[0]
    out = jnp.broadcast_to(v[:, 4 * g + 3:4 * g + 4], (rows, GROUP_W))
    for r in (2, 1, 0):
        out = jnp.where(width_iota_seg == r, v[:, 4 * g + r:4 * g + r + 1], out)
    return out


def _ssd_prompt_kernel(xbc_ref, dt_ref, z_ref, cw_ref, cb_ref, dtb_ref, alog_ref, dskip_ref, gn_ref,
                       y_ref, ssm_ref, cbuf, xc, st, ybuf):
    c = pl.program_id(1)
    ch = SSD_CHUNK
    pad = 8

    @pl.when(c == 0)
    def _():
        cbuf[0:pad, :] = jnp.zeros((pad, CONV_DIM), F32)
        st[...] = jnp.zeros_like(st)

    cbuf[pad:pad + ch, :] = xbc_ref[...]
    cstep = 512
    for j in range(0, CONV_DIM, cstep):
        sl = slice(j, j + cstep)
        acc = cb_ref[:, sl] + cw_ref[0:1, sl] * cbuf[pad - 3:pad - 3 + ch, sl]
        for i in range(1, SSD_CONV_W):
            acc = acc + cw_ref[i:i + 1, sl] * cbuf[pad - 3 + i:pad - 3 + i + ch, sl]
        xc[:, sl] = _silu(acc)
    cbuf[0:pad, :] = cbuf[ch:ch + pad, :]

    dt = _softplus(dt_ref[...] + dtb_ref[...])
    dta = dt * (-jnp.exp(alog_ref[...]))
    ti = _iota((ch, ch), 0)
    si = _iota((ch, ch), 1)
    causal = si <= ti
    acs = jnp.dot(causal.astype(F32), dta, preferred_element_type=F32,
                  precision=lax.Precision.HIGHEST)
    acs_t = acs.T
    eacs = jnp.exp(acs)
    de = jnp.exp(acs[ch - 1:ch, :] - acs)
    dtde = dt * de
    seg = _iota((ch, GROUP_W), 1) // SSD_HEAD_DIM

    for g in range(N_GROUPS):
        b_g = xc[:, D_INNER + g * D_STATE:D_INNER + (g + 1) * D_STATE].astype(BF16)
        c_g = xc[:, D_INNER + BC_DIM + g * D_STATE:D_INNER + BC_DIM + (g + 1) * D_STATE].astype(BF16)
        cb = _dot_nt(c_g, b_g)
        xs_g = xc[:, g * GROUP_W:(g + 1) * GROUP_W]
        xdt = (xs_g * _expand_heads(dt, g, seg)).astype(BF16)
        xde = (xs_g * _expand_heads(dtde, g, seg)).astype(BF16)
        h_g = st[g * GROUP_W:(g + 1) * GROUP_W, :]
        y_g = _dot_nt(c_g, h_g.astype(BF16)) * _expand_heads(eacs, g, seg)
        for r in range(HEADS_PER_GROUP):
            h = HEADS_PER_GROUP * g + r
            diff = acs[:, h:h + 1] - acs_t[h:h + 1, :]
            m = (cb * jnp.exp(jnp.where(causal, diff, -jnp.inf))).astype(BF16)
            y_g = y_g + jnp.where(seg == r, _dot(m, xdt), 0.0)
        ybuf[:, g * GROUP_W:(g + 1) * GROUP_W] = y_g
        upd = _dot_tn(xde, b_g)
        cd = jnp.concatenate(
            [jnp.broadcast_to(eacs[ch - 1:ch, HEADS_PER_GROUP * g + r:HEADS_PER_GROUP * g + r + 1],
                              (SSD_HEAD_DIM, D_STATE)) for r in range(HEADS_PER_GROUP)], axis=0)
        st[g * GROUP_W:(g + 1) * GROUP_W, :] = h_g * cd + upd

    y = ybuf[...] + dskip_ref[...] * xc[:, 0:D_INNER]
    yg = y * _silu(z_ref[...])
    y_ref[...] = yg * lax.rsqrt(jnp.mean(yg * yg, axis=-1, keepdims=True) + EPS) * gn_ref[...]

    @pl.when(c == pl.num_programs(1) - 1)
    def _():
        ssm_ref[...] = st[...]


def _ssd_prompt(proj, dt, conv_w, conv_b, dt_bias, a_log, d_skip, ssd_norm, l, batch, seq):
    ch = SSD_CHUNK
    nc = seq // ch
    z_blk = COL_Z // D_INNER

    def par(shape):
        return pl.BlockSpec((None,) + shape, lambda b, c: (l, 0, 0))

    return pl.pallas_call(
        _ssd_prompt_kernel,
        out_shape=[jax.ShapeDtypeStruct((batch * seq, D_INNER), F32),
                   jax.ShapeDtypeStruct((batch, N_SSD_HEADS * SSD_HEAD_DIM, D_STATE), F32)],
        grid=(batch, nc),
        in_specs=[pl.BlockSpec((ch, CONV_DIM), lambda b, c: (b * nc + c, COL_XBC // CONV_DIM)),
                  pl.BlockSpec((ch, LANES), lambda b, c: (b * nc + c, 0)),
                  pl.BlockSpec((ch, D_INNER), lambda b, c: (b * nc + c, z_blk)),
                  par((SSD_CONV_W, CONV_DIM)), par((1, CONV_DIM)), par((1, LANES)), par((1, LANES)),
                  par((1, D_INNER)), par((1, D_INNER))],
        out_specs=[pl.BlockSpec((ch, D_INNER), lambda b, c: (b * nc + c, 0)),
                   pl.BlockSpec((None, N_SSD_HEADS * SSD_HEAD_DIM, D_STATE), lambda b, c: (b, 0, 0))],
        scratch_shapes=[pltpu.VMEM((ch + 8, CONV_DIM), F32),
                        pltpu.VMEM((ch, CONV_DIM), F32),
                        pltpu.VMEM((N_SSD_HEADS * SSD_HEAD_DIM, D_STATE), F32),
                        pltpu.VMEM((ch, D_INNER), F32)],
        compiler_params=_params("arbitrary", "arbitrary"),
        name="ssd_prompt",
    )(proj, dt, proj, conv_w, conv_b, dt_bias, a_log, d_skip, ssd_norm)


_QB = 256
_PAIRS = 4
_HW = _PAIRS * LANES


def _sb_prompt_kernel(q_ref, k_ref, v_ref, b_ref, o_ref, kt_ref, vb_ref, acc_ref, carry_ref):
    qi = pl.program_id(2)
    blk = _QB
    n_kb = k_ref.shape[0] // blk

    @pl.when(qi == 0)
    def _():
        for j in range(n_kb):
            kt_ref[j] = k_ref[j * blk:(j + 1) * blk, :].T.astype(BF16)
            vb_ref[j] = v_ref[j * blk:(j + 1) * blk, :].astype(BF16)

    first_head = _iota((blk, LANES), 1) < ATT_HEAD_DIM
    jj = _iota((blk, blk), 0)
    ss = _iota((blk, blk), 1)
    strictly_lower = ss < jj
    later = (jj > ss).astype(BF16)
    qm = []
    for pi in range(_PAIRS):
        q2 = q_ref[:, pi * LANES:(pi + 1) * LANES] * (ATT_HEAD_DIM ** -0.5)
        qm.append(jnp.where(first_head, q2, 0.0).astype(BF16))
        qm.append(jnp.where(first_head, 0.0, q2).astype(BF16))
    acc_ref[...] = jnp.zeros_like(acc_ref)
    carry_ref[...] = jnp.zeros_like(carry_ref)

    def block(kj, masked):
        heads = range(2 * _PAIRS)
        cols = [slice((h // 2) * LANES, (h // 2 + 1) * LANES) for h in heads]
        z = [_dot(qm[h], kt_ref[kj, cols[h], :]) + b_ref[h:h + 1, :] for h in heads]
        sp = [jnp.maximum(z[h], 0.0) + jnp.log(1.0 + jnp.exp(-jnp.abs(z[h]))) for h in heads]
        if masked:
            sp = [jnp.where(strictly_lower, sp[h], 0.0) for h in heads]
        parts = [_split2(sp[h]) for h in heads]
        suffix = [_dot(parts[h][0], later) + _dot(parts[h][1], later) for h in heads]
        carry = [carry_ref[h] for h in heads]
        w = [jnp.exp(z[h] - sp[h] - suffix[h] - jnp.concatenate([carry[h], carry[h]], axis=1)) for h in heads]
        if masked:
            w = [jnp.where(strictly_lower, w[h], 0.0) for h in heads]
        pv = [_dot(w[h].astype(BF16), vb_ref[kj, :, cols[h]]) for h in heads]
        for h in heads:
            acc_ref[h] += pv[h]
            carry_ref[h] = carry[h] + jnp.broadcast_to(suffix[h][:, 0:1] + sp[h][:, 0:1], (blk, LANES))

    block(qi, True)

    def body(t, c):
        block(qi - 1 - t, False)
        return c

    lax.fori_loop(0, qi, body, 0)
    for pi in range(_PAIRS):
        o_ref[:, pi * LANES:(pi + 1) * LANES] = jnp.where(first_head, acc_ref[2 * pi], acc_ref[2 * pi + 1])


def _sb_prompt(proj, bias2, l, batch, seq):
    nq = seq // _QB
    groups = D_ATT // _HW
    return pl.pallas_call(
        _sb_prompt_kernel,
        out_shape=jax.ShapeDtypeStruct((batch * seq, D_ATT), F32),
        grid=(batch, groups, nq),
        in_specs=[pl.BlockSpec((_QB, _HW), lambda b, g, i: (b * nq + i, COL_Q // _HW + g)),
                  pl.BlockSpec((seq, _HW), lambda b, g, i: (b, COL_K // _HW + g)),
                  pl.BlockSpec((seq, _HW), lambda b, g, i: (b, COL_V // _HW + g)),
                  pl.BlockSpec((None, None, 2 * _PAIRS, _QB), lambda b, g, i: (l, g, 0, 0))],
        out_specs=pl.BlockSpec((_QB, _HW), lambda b, g, i: (b * nq + i, g)),
        scratch_shapes=[pltpu.VMEM((nq, _HW, _QB), BF16),
                        pltpu.VMEM((nq, _QB, _HW), BF16),
                        pltpu.VMEM((2 * _PAIRS, _QB, LANES), F32),
                        pltpu.VMEM((2 * _PAIRS, _QB, LANES), F32)],
        compiler_params=_params("arbitrary", "arbitrary", "arbitrary"),
        name="sb_prompt",
    )(proj, proj, proj, bias2)


_TSLOT = 8
_HGRP = 4


_PPS = 4


def _sb_sample_kernel(pt_ref, qbd_ref, qrow_ref, kn_ref, vn_ref, brep_ref, *refs):
    del pt_ref
    k_refs, v_refs = refs[0:_PPS], refs[_PPS:2 * _PPS]
    out_ref, acc_ref, carry_ref, onew_ref = refs[2 * _PPS:]
    p = pl.program_id(1)
    t_new = kn_ref.shape[0]
    page = k_refs[0].shape[1]
    n_grp = N_ATT_HEADS // _HGRP
    grp = _HGRP * _TSLOT
    gw = _HGRP * ATT_HEAD_DIM
    scale = ATT_HEAD_DIM ** -0.5

    @pl.when(p == 0)
    def _():
        acc_ref[...] = jnp.zeros_like(acc_ref)
        q = qrow_ref[...]
        bias = brep_ref[:, 0:1]
        t_of_row = _iota((LANES, 1), 0) % _TSLOT
        z, sp, vis = [], [], []
        for s in range(t_new):
            z.append(jnp.sum(q * kn_ref[s], axis=1, keepdims=True) * scale + bias)
            sp.append(_softplus(z[s]))
            vis.append(s < t_of_row)
        suffix = jnp.zeros((LANES, 1), F32)
        o_new = jnp.zeros((LANES, LANES), F32)
        for s in reversed(range(t_new)):
            w = jnp.where(vis[s], jnp.exp(z[s] - sp[s] - suffix), 0.0)
            o_new = o_new + w * vn_ref[s]
            suffix = suffix + jnp.where(vis[s], sp[s], 0.0)
        onew_ref[...] = o_new
        carry_ref[...] = jnp.broadcast_to(suffix, (LANES, LANES))

    @pl.when(p > 0)
    def _():
        jj = _iota((page, 2 * page), 0)
        ss = _iota((page, 2 * page), 1)
        tri = ((jj > ss) | (ss >= page)).astype(BF16)
        tri2 = jnp.concatenate([tri, tri], axis=0)
        qbd = qbd_ref[...]
        slots = range(_PPS)
        kt = [k_refs[i][...].astype(BF16) for i in slots]
        z = [jnp.concatenate(
            [_dot(qbd[grp * g:grp * (g + 1), gw * g:gw * (g + 1)], kt[i][gw * g:gw * (g + 1), :])
             for g in range(n_grp)], axis=0) * scale + brep_ref[...] for i in slots]
        sp = [_softplus(z[i]) for i in slots]
        s2 = [_dot(jnp.concatenate(_split2(sp[i]), axis=1), tri2) for i in slots]
        carry = [carry_ref[...]]
        for i in slots:
            carry.append(carry[i] + s2[i][:, page:2 * page])
        wb = [jnp.exp(z[i] - sp[i] - s2[i][:, 0:page] - carry[i]).astype(BF16) for i in slots]
        vt = [v_refs[i][...].astype(BF16) for i in slots]
        for g in range(n_grp):
            pv = [_dot_nt(wb[i][grp * g:grp * (g + 1), :], vt[i][gw * g:gw * (g + 1), :]) for i in slots]
            acc_ref[g] += sum(pv[1:], pv[0])
        carry_ref[...] = carry[_PPS]

    @pl.when(p == pl.num_programs(1) - 1)
    def _():
        rr = _iota((LANES, gw), 0)
        cc = _iota((LANES, gw), 1)
        rep = ((rr < ATT_HEAD_DIM) & (cc % ATT_HEAD_DIM == rr)).astype(BF16)
        for g in range(n_grp):
            parts = _split3(onew_ref[grp * g:grp * (g + 1), :])
            out_ref[g] = acc_ref[g] + _dot(parts[0], rep) + _dot(parts[1], rep) + _dot(parts[2], rep)


def _sb_sample(page_table, qbd, qrow, kn_exp, vn_exp, brep, cache_kt, cache_vt, l):
    nb = qbd.shape[0]
    t_new = kn_exp.shape[1]
    n_pages = page_table.shape[1]
    page = cache_kt.shape[3]
    n_grp = N_ATT_HEADS // _HGRP

    def page_map(slot):
        def index_map(b, p, pt):
            first = n_pages - 1 - _PPS * (jnp.maximum(p, 1) - 1)
            return (l, pt[b, first - slot], 0, 0)
        return pl.BlockSpec((None, None, D_ATT, page), index_map)

    def per_seq(shape):
        return pl.BlockSpec((None,) + shape, lambda b, p, pt: (b,) + (0,) * len(shape))

    grid_spec = pltpu.PrefetchScalarGridSpec(
        num_scalar_prefetch=1,
        grid=(nb, n_pages // _PPS + 1),
        in_specs=[per_seq((LANES, D_ATT)), per_seq((LANES, ATT_HEAD_DIM)),
                  per_seq((t_new, LANES, ATT_HEAD_DIM)), per_seq((t_new, LANES, LANES)),
                  pl.BlockSpec((None, LANES, LANES), lambda b, p, pt: (l, 0, 0))]
        + [page_map(s) for s in range(_PPS)] * 2,
        out_specs=per_seq((n_grp, _HGRP * _TSLOT, _HGRP * ATT_HEAD_DIM)),
        scratch_shapes=[pltpu.VMEM((n_grp, _HGRP * _TSLOT, _HGRP * ATT_HEAD_DIM), F32),
                        pltpu.VMEM((LANES, LANES), F32),
                        pltpu.VMEM((LANES, LANES), F32)])
    return pl.pallas_call(
        _sb_sample_kernel,
        out_shape=jax.ShapeDtypeStruct((nb, n_grp, _HGRP * _TSLOT, _HGRP * ATT_HEAD_DIM), F32),
        grid_spec=grid_spec,
        compiler_params=_params("arbitrary", "arbitrary"),
        name="sb_sample",
    )(page_table, qbd, qrow, kn_exp, vn_exp, brep, *([cache_kt] * _PPS + [cache_vt] * _PPS))


def _ssd_sample_pre_kernel(xbc_ref, prev_ref, dt_ref, cw_ref, cb_ref, dtb_ref, alog_ref, seg_ref, exp_ref,
                           xc_ref, yd_ref, xd_ref, ea_ref, cd_ref):
    t_new = xbc_ref.shape[0]
    taps = SSD_CONV_W

    def tap(r, sl):
        return prev_ref[r, :, sl] if r < taps - 1 else xbc_ref[r - (taps - 1), :, sl]

    cstep = 512
    for t in range(t_new):
        for j in range(0, CONV_DIM, cstep):
            sl = slice(j, j + cstep)
            acc = cb_ref[:, sl] + cw_ref[0:1, sl] * tap(t, sl)
            for i in range(1, taps):
                acc = acc + cw_ref[i:i + 1, sl] * tap(t + i, sl)
            xc_ref[t, :, sl] = _silu(acc)

    a = -jnp.exp(alog_ref[...])
    dt = [_softplus(dt_ref[t] + dtb_ref[...]) for t in range(t_new)]
    acs = []
    for t in range(t_new):
        step = dt[t] * a
        acs.append(step if t == 0 else acs[t - 1] + step)

    def expand(v):
        e = exp_ref[...]
        hi, mid, lo = _split3(v)
        return _dot(hi, e) + _dot(mid, e) + _dot(lo, e)

    seg = seg_ref[...]
    for t in range(t_new):
        c_t = xc_ref[t, :, D_INNER + BC_DIM:CONV_DIM]
        yd = None
        for s in range(t + 1):
            b_s = xc_ref[s, :, D_INNER:D_INNER + BC_DIM]
            hi, lo = _split2(c_t * b_s)
            cbh = _dot(hi, seg) + _dot(lo, seg)
            coef = cbh * jnp.exp(acs[t] - acs[s]) * dt[s]
            term = expand(coef) * xc_ref[s, :, 0:D_INNER]
            yd = term if yd is None else yd + term
        yd_ref[t] = yd
        ea_ref[t] = expand(jnp.exp(acs[t]))
        xd_ref[t] = expand(dt[t] * jnp.exp(acs[t_new - 1] - acs[t])) * xc_ref[t, :, 0:D_INNER]
    cd_ref[...] = jnp.exp(acs[t_new - 1])


def _ssd_sample_pre(xbc_tm, prev_tm, dt_tm, conv_w, conv_b, dt_bias, a_log, seg_mat, exp_mat, l):
    t_new, nb, _ = xbc_tm.shape
    bt = 32

    def par(shape):
        return pl.BlockSpec((None,) + shape, lambda i: (l, 0, 0))

    def tm_spec(t, w):
        return pl.BlockSpec((t, bt, w), lambda i: (0, i, 0))

    return pl.pallas_call(
        _ssd_sample_pre_kernel,
        out_shape=[jax.ShapeDtypeStruct((t_new, nb, CONV_DIM), F32),
                   jax.ShapeDtypeStruct((t_new, nb, D_INNER), F32),
                   jax.ShapeDtypeStruct((t_new, nb, D_INNER), F32),
                   jax.ShapeDtypeStruct((t_new, nb, D_INNER), F32),
                   jax.ShapeDtypeStruct((nb, LANES), F32)],
        grid=(nb // bt,),
        in_specs=[tm_spec(t_new, CONV_DIM), tm_spec(SSD_CONV_W - 1, CONV_DIM), tm_spec(t_new, LANES),
                  par((SSD_CONV_W, CONV_DIM)), par((1, CONV_DIM)), par((1, LANES)), par((1, LANES)),
                  pl.BlockSpec((BC_DIM, LANES), lambda i: (0, 0)),
                  pl.BlockSpec((LANES, D_INNER), lambda i: (0, 0))],
        out_specs=[tm_spec(t_new, CONV_DIM), tm_spec(t_new, D_INNER), tm_spec(t_new, D_INNER),
                   tm_spec(t_new, D_INNER), pl.BlockSpec((bt, LANES), lambda i: (i, 0))],
        compiler_params=_params("arbitrary"),
        name="ssd_sample_pre",
    )(xbc_tm, prev_tm, dt_tm, conv_w, conv_b, dt_bias, a_log, seg_mat, exp_mat)


_TPAD = 16


def _ssd_sample_state_kernel(st_ref, bc_ref, xd_ref, cd_ref, new_ref, yoff_ref):
    bb = st_ref.shape[0]
    for jb in range(bb):
        bc = bc_ref[jb]
        xd = xd_ref[jb].astype(BF16)
        cdb = cd_ref[jb]
        for g in range(N_GROUPS):
            rows = slice(g * GROUP_W, (g + 1) * GROUP_W)
            h_g = st_ref[jb, rows, :]
            b_g = bc[:, g * D_STATE:(g + 1) * D_STATE].astype(BF16)
            c_g = bc[:, BC_DIM + g * D_STATE:BC_DIM + (g + 1) * D_STATE].astype(BF16)
            yoff_ref[jb, :, rows] = _dot_nt(c_g, h_g.astype(BF16))
            upd = _dot_tn(xd[:, rows], b_g)
            cd = jnp.concatenate(
                [jnp.broadcast_to(cdb[0:1, HEADS_PER_GROUP * g + r:HEADS_PER_GROUP * g + r + 1],
                                  (SSD_HEAD_DIM, D_STATE)) for r in range(HEADS_PER_GROUP)], axis=0)
            new_ref[jb, rows, :] = h_g * cd + upd


def _ssd_sample_state(state_ssm, bc_bm, xd_bm, cd, l):
    nb = bc_bm.shape[0]
    bb = 4
    rows = N_SSD_HEADS * SSD_HEAD_DIM
    return pl.pallas_call(
        _ssd_sample_state_kernel,
        out_shape=[jax.ShapeDtypeStruct((nb, rows, D_STATE), F32),
                   jax.ShapeDtypeStruct((nb, _TPAD, D_INNER), F32)],
        grid=(nb // bb,),
        in_specs=[pl.BlockSpec((None, bb, rows, D_STATE), lambda i: (l, i, 0, 0)),
                  pl.BlockSpec((bb, _TPAD, 2 * BC_DIM), lambda i: (i, 0, 0)),
                  pl.BlockSpec((bb, _TPAD, D_INNER), lambda i: (i, 0, 0)),
                  pl.BlockSpec((bb, 1, LANES), lambda i: (i, 0, 0))],
        out_specs=[pl.BlockSpec((bb, rows, D_STATE), lambda i: (i, 0, 0)),
                   pl.BlockSpec((bb, _TPAD, D_INNER), lambda i: (i, 0, 0))],
        compiler_params=_params("arbitrary"),
        name="ssd_sample_state",
    )(state_ssm, bc_bm, xd_bm, cd)


def _ssd_sample_post_kernel(yd_ref, ea_ref, yoff_ref, xs_ref, z_ref, dskip_ref, gn_ref, y_ref):
    y = yd_ref[...] + ea_ref[...] * yoff_ref[...] + dskip_ref[...] * xs_ref[...]
    yg = y * _silu(z_ref[...])
    y_ref[...] = yg * lax.rsqrt(jnp.mean(yg * yg, axis=-1, keepdims=True) + EPS) * gn_ref[...]


def _ssd_sample_post(yd, ea, yoff, xc, proj, d_skip, ssd_norm, l):
    n = yd.shape[0]
    tm = 128
    row = pl.BlockSpec((tm, D_INNER), lambda i: (i, 0))
    z_row = pl.BlockSpec((tm, D_INNER), lambda i: (i, COL_Z // D_INNER))
    par = pl.BlockSpec((None, 1, D_INNER), lambda i: (l, 0, 0))
    return pl.pallas_call(
        _ssd_sample_post_kernel,
        out_shape=jax.ShapeDtypeStruct((n, D_INNER), F32),
        grid=(n // tm,),
        in_specs=[row, row, row, row, z_row, par, par],
        out_specs=row,
        compiler_params=_params("arbitrary"),
        name="ssd_sample_post",
    )(yd, ea, yoff, xc, proj, d_skip, ssd_norm)


def _mix_kernel(y_ref, o_ref, gs_ref, ga_ref, x_ref, g1_ref, wbs_ref, wba_ref, wo_ref, out_ref):
    a = _dot(y_ref[...].astype(BF16), wbs_ref[...])
    b = _dot(o_ref[...].astype(BF16), wba_ref[...])
    merged = jax.nn.sigmoid(gs_ref[...]) * a + jax.nn.sigmoid(ga_ref[...]) * b
    mix = _dot(merged.astype(BF16), wo_ref[...])
    out_ref[...] = x_ref[...] + g1_ref[...] * mix


def _mix(y, o, proj, x, mod, tiles_per_mod, w_br_ssd, w_br_att, w_out, l, tm):
    n = x.shape[0]
    mod_rows = mod.shape[2]

    def row(w, col=0):
        return pl.BlockSpec((tm, w), lambda i: (i, col // w))

    def wspec(k):
        return pl.BlockSpec((None, k, D_MODEL), lambda i: (l, 0, 0))

    return pl.pallas_call(
        _mix_kernel,
        out_shape=jax.ShapeDtypeStruct((n, D_MODEL), F32),
        grid=(n // tm,),
        in_specs=[row(D_INNER), row(D_ATT), row(D_MODEL, COL_GS), row(D_MODEL, COL_GA), row(D_MODEL),
                  pl.BlockSpec((None, None, mod_rows, D_MODEL), lambda i: (l, i // tiles_per_mod, 0, 2)),
                  wspec(D_INNER), wspec(D_ATT), wspec(D_MODEL)],
        out_specs=row(D_MODEL),
        compiler_params=_params("arbitrary"),
        name="mix",
    )(y, o, proj, proj, x, mod, w_br_ssd, w_br_att, w_out)


_FFN_STEP = 256


def _ffn_gate(tap_rows, cw_ref, cb_ref, act_ref):
    for j in range(0, D_FF, _FFN_STEP):
        halves = []
        for off in (0, D_FF):
            sl = slice(off + j, off + j + _FFN_STEP)
            acc = cb_ref[:, sl] + cw_ref[0:1, sl] * tap_rows(0, sl)
            for i in range(1, FFN_CONV_W):
                acc = acc + cw_ref[i:i + 1, sl] * tap_rows(i, sl)
            halves.append(acc)
        val, gate = halves
        act_ref[:, j:j + _FFN_STEP] = (_silu(gate) * val).astype(BF16)


def _ffn_prompt_kernel(u_ref, x_ref, g2_ref, cw_ref, cb_ref, wd_ref, out_ref, cbuf, act):
    tm = u_ref.shape[0]
    pad = 8

    @pl.when(pl.program_id(1) == 0)
    def _():
        cbuf[0:pad, :] = jnp.zeros((pad, 2 * D_FF), F32)

    cbuf[pad:pad + tm, :] = u_ref[...]
    first = pad - (FFN_CONV_W - 1)
    _ffn_gate(lambda i, sl: cbuf[first + i:first + i + tm, sl], cw_ref, cb_ref, act)
    cbuf[0:pad, :] = cbuf[tm:tm + pad, :]
    out_ref[...] = x_ref[...] + g2_ref[...] * _dot(act[...], wd_ref[...])


def _ffn_prompt(u, x, mod, ffn_conv_w, ffn_conv_b, w_down, l, batch, seq, tm):
    nt = seq // tm

    def par(shape):
        return pl.BlockSpec((None,) + shape, lambda b, c: (l, 0, 0))

    return pl.pallas_call(
        _ffn_prompt_kernel,
        out_shape=jax.ShapeDtypeStruct((batch * seq, D_MODEL), F32),
        grid=(batch, nt),
        in_specs=[pl.BlockSpec((tm, 2 * D_FF), lambda b, c: (b * nt + c, 0)),
                  pl.BlockSpec((tm, D_MODEL), lambda b, c: (b * nt + c, 0)),
                  pl.BlockSpec((None, None, 1, D_MODEL), lambda b, c: (l, b, 0, 5)),
                  par((FFN_CONV_W, 2 * D_FF)), par((1, 2 * D_FF)), par((D_FF, D_MODEL))],
        out_specs=pl.BlockSpec((tm, D_MODEL), lambda b, c: (b * nt + c, 0)),
        scratch_shapes=[pltpu.VMEM((tm + 8, 2 * D_FF), F32), pltpu.VMEM((tm, D_FF), BF16)],
        compiler_params=_params("arbitrary", "arbitrary"),
        name="ffn_prompt",
    )(u, x, mod, ffn_conv_w, ffn_conv_b, w_down)


def _ffn_sample_kernel(u_ref, prev_ref, x_ref, g2_ref, cw_ref, cb_ref, wd_ref, out_ref, act):
    t_new, nb, _ = u_ref.shape
    n_prev = FFN_CONV_W - 1
    for t in range(t_new):
        def tap_rows(i, sl, t=t):
            src = t + i - n_prev
            return u_ref[src, :, sl] if src >= 0 else prev_ref[src + n_prev, :, sl]

        _ffn_gate(tap_rows, cw_ref, cb_ref, act.at[t * nb:(t + 1) * nb])
    out_ref[...] = x_ref[...] + g2_ref[...] * _dot(act[...], wd_ref[...])


def _ffn_sample(u_tm, prev_tm, x, mod, ffn_conv_w, ffn_conv_b, w_down, l):
    t_new, nb, _ = u_tm.shape
    n = t_new * nb

    def par(shape):
        return pl.BlockSpec((None,) + shape, lambda i: (l, 0, 0))

    return pl.pallas_call(
        _ffn_sample_kernel,
        out_shape=jax.ShapeDtypeStruct((n, D_MODEL), F32),
        grid=(1,),
        in_specs=[pl.BlockSpec((t_new, nb, 2 * D_FF), lambda i: (0, 0, 0)),
                  pl.BlockSpec((FFN_CONV_W - 1, nb, 2 * D_FF), lambda i: (0, 0, 0)),
                  pl.BlockSpec((n, D_MODEL), lambda i: (0, 0)),
                  pl.BlockSpec((None, None, n, D_MODEL), lambda i: (l, 0, 0, 5)),
                  par((FFN_CONV_W, 2 * D_FF)), par((1, 2 * D_FF)), par((D_FF, D_MODEL))],
        out_specs=pl.BlockSpec((n, D_MODEL), lambda i: (0, 0)),
        scratch_shapes=[pltpu.VMEM((n, D_FF), BF16)],
        compiler_params=_params("arbitrary"),
        name="ffn_sample",
    )(u_tm, prev_tm, x, mod, ffn_conv_w, ffn_conv_b, w_down)


def _final_norm_kernel(x_ref, g_ref, o_ref):
    x = x_ref[...]
    o_ref[...] = x * lax.rsqrt(jnp.mean(x * x, axis=-1, keepdims=True) + EPS) * g_ref[...]


def _final_norm(x, g, tm):
    n = x.shape[0]
    return pl.pallas_call(
        _final_norm_kernel,
        out_shape=jax.ShapeDtypeStruct((n, D_MODEL), F32),
        grid=(n // tm,),
        in_specs=[pl.BlockSpec((tm, D_MODEL), lambda i: (i, 0)),
                  pl.BlockSpec((1, D_MODEL), lambda i: (0, 0))],
        out_specs=pl.BlockSpec((tm, D_MODEL), lambda i: (i, 0)),
        compiler_params=_params("arbitrary"),
        name="final_norm",
    )(x, g.reshape(1, D_MODEL))


def _to_batch_major(a_tm, t_new, nb):
    return a_tm.reshape(t_new, nb, -1).transpose(1, 0, 2)


def _pad_time(a_bm):
    return jnp.pad(a_bm, ((0, 0), (0, _TPAD - a_bm.shape[1]), (0, 0)))


def kernel(x_prompt, x_sample, cache_k, cache_v, state_ssm, state_conv, state_ffn_conv, page_table,
           c_prompt, c_sample, w_ada, b_ada, norm_mix, norm_ffn, w_in, conv_w, conv_b, dt_bias, a_log,
           d_skip, ssd_norm, sb_bias, w_br_ssd, w_br_att, w_out, w_up, ffn_conv_w, ffn_conv_b, w_down,
           norm_final):
    batch, seq, _ = x_prompt.shape
    nb, t_new, _ = x_sample.shape
    depth = w_in.shape[0]
    n_s = nb * t_new
    tm_in = min(seq, 1024)
    tm_up = min(seq, 1024)

    mods = _ada(jnp.concatenate([c_prompt, c_sample], axis=0), w_ada, b_ada)
    mod_p = mods[:, :batch].reshape(depth, batch, 1, N_MOD * D_MODEL)
    mod_s = jnp.tile(mods[:, batch:], (1, t_new, 1)).reshape(depth, 1, n_s, N_MOD * D_MODEL)

    cuts = (D_INNER, D_INNER + CONV_DIM)
    w_main = jnp.concatenate([w_in[:, :, cuts[0]:cuts[1]], w_in[:, :, :cuts[0]],
                              w_in[:, :, cuts[1] + N_SSD_HEADS:]], axis=2).astype(BF16)
    w_dt = jnp.pad(w_in[:, :, cuts[1]:cuts[1] + N_SSD_HEADS],
                   ((0, 0), (0, 0), (0, LANES - N_SSD_HEADS))).astype(BF16)
    w_br_ssd_b = w_br_ssd.astype(BF16)
    w_br_att_b = w_br_att.astype(BF16)
    w_out_b = w_out.astype(BF16)
    w_up_b = w_up.astype(BF16)
    w_down_b = w_down.astype(BF16)

    def row3(a):
        return a.reshape(depth, 1, a.shape[-1])

    def head_row(a):
        return jnp.pad(a, ((0, 0), (0, LANES - N_SSD_HEADS))).reshape(depth, 1, LANES)

    norm_mix3, norm_ffn3 = row3(norm_mix), row3(norm_ffn)
    conv_b3, ffn_conv_b3, ssd_norm3 = row3(conv_b), row3(ffn_conv_b), row3(ssd_norm)
    dt_bias3, a_log3 = head_row(dt_bias), head_row(a_log)
    d_skip3 = row3(jnp.repeat(d_skip, SSD_HEAD_DIM, axis=1))
    bias2 = jnp.broadcast_to(sb_bias[:, :, None], (depth, N_ATT_HEADS, _QB)).reshape(
        depth, N_ATT_HEADS // (2 * _PAIRS), 2 * _PAIRS, _QB)
    brep = jnp.broadcast_to(jnp.repeat(sb_bias, _TSLOT, axis=1)[:, :, None], (depth, LANES, LANES))
    head_eye = jnp.eye(N_ATT_HEADS, dtype=bool)
    seg_mat = (jnp.arange(BC_DIM)[:, None] // D_STATE
               == jnp.where(jnp.arange(LANES) < N_SSD_HEADS, jnp.arange(LANES) // HEADS_PER_GROUP, -1)[None, :]
               ).astype(BF16)
    exp_mat = (jnp.arange(LANES)[:, None] == jnp.arange(D_INNER)[None, :] // SSD_HEAD_DIM).astype(BF16)

    cache_kt = cache_k.transpose(0, 1, 3, 4, 2).reshape(cache_k.shape[:2] + (D_ATT, cache_k.shape[2]))
    cache_vt = cache_v.transpose(0, 1, 3, 4, 2).reshape(cache_v.shape[:2] + (D_ATT, cache_v.shape[2]))
    state_ssm4 = state_ssm.reshape(depth, nb, N_SSD_HEADS * SSD_HEAD_DIM, D_STATE)

    xp = x_prompt.reshape(batch * seq, D_MODEL)
    xs = x_sample.transpose(1, 0, 2).reshape(n_s, D_MODEL)
    outs = {name: [] for name in ("kp", "vp", "sp", "cp", "fp", "ks", "vs", "ss", "cs", "fs")}

    for l in range(depth):
        proj, dt = _in_proj(xp, mod_p, seq // tm_in, norm_mix3, w_main, w_dt, l, tm_in)
        y, ssm_new = _ssd_prompt(proj, dt, conv_w, conv_b3, dt_bias3, a_log3, d_skip3, ssd_norm3,
                                 l, batch, seq)
        o = _sb_prompt(proj, bias2, l, batch, seq)
        xp = _mix(y, o, proj, xp, mod_p, seq // 256, w_br_ssd_b, w_br_att_b, w_out_b, l, 256)
        u = _up_proj(xp, mod_p, seq // tm_up, norm_ffn3, w_up_b, l, tm_up)
        xp = _ffn_prompt(u, xp, mod_p, ffn_conv_w, ffn_conv_b3, w_down_b, l, batch, seq, 256)
        proj3 = proj.reshape(batch, seq, PROJ_COLS)
        outs["kp"].append(proj3[:, :, COL_K:COL_K + D_ATT].reshape(batch, seq, N_ATT_HEADS, ATT_HEAD_DIM))
        outs["vp"].append(proj3[:, :, COL_V:COL_V + D_ATT].reshape(batch, seq, N_ATT_HEADS, ATT_HEAD_DIM))
        outs["sp"].append(ssm_new.reshape(batch, N_SSD_HEADS, SSD_HEAD_DIM, D_STATE))
        outs["cp"].append(proj3[:, seq - (SSD_CONV_W - 1):, COL_XBC:COL_XBC + CONV_DIM])
        outs["fp"].append(u.reshape(batch, seq, 2 * D_FF)[:, seq - (FFN_CONV_W - 1):])

        proj, dt = _in_proj(xs, mod_s, 1, norm_mix3, w_main, w_dt, l, n_s)
        q, k, v = (proj[:, c:c + D_ATT] for c in (COL_Q, COL_K, COL_V))
        xbc_tm = proj[:, COL_XBC:COL_XBC + CONV_DIM].reshape(t_new, nb, CONV_DIM)
        xc, yd, xd, ea, cd = _ssd_sample_pre(
            xbc_tm, state_conv[l].transpose(1, 0, 2), dt.reshape(t_new, nb, LANES),
            conv_w, conv_b3, dt_bias3, a_log3, seg_mat, exp_mat, l)
        bc_bm = _pad_time(xc[:, :, D_INNER:].transpose(1, 0, 2))
        xd_bm = _pad_time(xd.transpose(1, 0, 2))
        ssm_new, yoff_bm = _ssd_sample_state(state_ssm4, bc_bm, xd_bm, cd.reshape(nb, 1, LANES), l)
        yoff = yoff_bm[:, :t_new].transpose(1, 0, 2).reshape(n_s, D_INNER)
        y = _ssd_sample_post(yd.reshape(n_s, D_INNER), ea.reshape(n_s, D_INNER), yoff,
                             xc.reshape(n_s, CONV_DIM), proj, d_skip3, ssd_norm3, l)
        k_bm, v_bm = (_to_batch_major(a, t_new, nb) for a in (k, v))
        q5 = jnp.pad(q.reshape(t_new, nb, N_ATT_HEADS, ATT_HEAD_DIM).transpose(1, 2, 0, 3),
                     ((0, 0), (0, 0), (0, _TSLOT - t_new), (0, 0)))
        qrow = q5.reshape(nb, LANES, ATT_HEAD_DIM)
        qbd = jnp.where(head_eye[None, :, None, :, None], q5[:, :, :, None, :], 0.0).reshape(
            nb, LANES, D_ATT).astype(BF16)
        kn_exp, vn_exp = (
            jnp.broadcast_to(a.reshape(nb, t_new, N_ATT_HEADS, 1, ATT_HEAD_DIM),
                             (nb, t_new, N_ATT_HEADS, _TSLOT, ATT_HEAD_DIM)).reshape(
                                 nb, t_new, LANES, ATT_HEAD_DIM) for a in (k_bm, v_bm))
        vn_exp = jnp.pad(vn_exp, ((0, 0), (0, 0), (0, 0), (0, LANES - ATT_HEAD_DIM)))
        o_raw = _sb_sample(page_table, qbd, qrow, kn_exp, vn_exp, brep, cache_kt, cache_vt, l)
        o_raw = o_raw.reshape(nb, N_ATT_HEADS // _HGRP, _HGRP, _TSLOT, _HGRP, ATT_HEAD_DIM)
        o_bm = jnp.moveaxis(jnp.diagonal(o_raw, axis1=2, axis2=4), -1, 2)[:, :, :, :t_new]
        o = o_bm.reshape(nb, N_ATT_HEADS, t_new, ATT_HEAD_DIM).transpose(2, 0, 1, 3).reshape(n_s, D_ATT)
        xs = _mix(y, o, proj, xs, mod_s, 1, w_br_ssd_b, w_br_att_b, w_out_b, l, n_s)
        u = _up_proj(xs, mod_s, 1, norm_ffn3, w_up_b, l, n_s)
        u_tm = u.reshape(t_new, nb, 2 * D_FF)
        xs = _ffn_sample(u_tm, state_ffn_conv[l].transpose(1, 0, 2), xs, mod_s,
                         ffn_conv_w, ffn_conv_b3, w_down_b, l)
        outs["ks"].append(k_bm.reshape(nb, t_new, N_ATT_HEADS, ATT_HEAD_DIM))
        outs["vs"].append(v_bm.reshape(nb, t_new, N_ATT_HEADS, ATT_HEAD_DIM))
        outs["ss"].append(ssm_new.reshape(nb, N_SSD_HEADS, SSD_HEAD_DIM, D_STATE))
        outs["cs"].append(xbc_tm[t_new - (SSD_CONV_W - 1):].transpose(1, 0, 2))
        outs["fs"].append(u_tm[t_new - (FFN_CONV_W - 1):].transpose(1, 0, 2))

    y_prompt = _final_norm(xp, norm_final, 512).reshape(batch, seq, D_MODEL)
    y_sample = _final_norm(xs, norm_final, n_s).reshape(t_new, nb, D_MODEL).transpose(1, 0, 2)
    st = {name: jnp.stack(vals) for name, vals in outs.items()}
    return (y_prompt, y_sample, st["kp"], st["vp"], st["sp"], st["cp"], st["fp"],
            st["ks"], st["vs"], st["ss"], st["cs"], st["fs"])
```
